```python
import math
import jax, jax.numpy as jnp
from jax import lax
import numpy as np

D_MODEL = 1024
BATCH = 16
SEQ = 2048
DEPTH = 4
DEC_BATCH = 128
DEC_SEQ = 8
PAST_LEN = 8192
PAGE_SIZE = 128

DA_HEADS = 4
DA_HD = 64
DA_VD = 2 * DA_HD
DA_ROT = DA_HD // 4
DA_THETA = 500000.0
CONV_C = 512
CONV_W = 31
MLA_HEADS = 8
MLA_NOPE = 64
MLA_ROPE = 32
MLA_V = 64
MLA_Q_LORA = 384
MLA_KV_LORA = 256
MLA_THETA = 10000.0
N_BRANCH = 3
BRANCH_W = 512
N_EXPERTS = 32
TOP_K = 4
D_FF = D_MODEL
SWIGLU_LIMIT = 7.0
SWIGLU_ALPHA = 1.702
MOE_BLOCK = 128
Q_BLOCK = 128
SEQ_GROUP = 8
NORM_EPS = 1e-6
IN_WIDTHS = (DA_HEADS * 2 * DA_HD, 2 * DA_HD, DA_VD, 2 * CONV_C, MLA_Q_LORA, MLA_KV_LORA, MLA_ROPE, N_BRANCH * D_MODEL)
N_IN = sum(IN_WIDTHS)

kernel_name = "hybrid_diffattn_conformer_mla_moe_step"


def _rms(x, g):
    xf = x.astype(jnp.float32)
    y = xf * lax.rsqrt(jnp.mean(xf * xf, axis=-1, keepdims=True) + NORM_EPS)
    return (y * g.astype(jnp.float32)).astype(x.dtype)


def _layernorm(x, g, b):
    xf = x.astype(jnp.float32)
    mu = jnp.mean(xf, axis=-1, keepdims=True)
    xc = xf - mu
    y = xc * lax.rsqrt(jnp.mean(xc * xc, axis=-1, keepdims=True) + NORM_EPS)
    return (y * g.astype(jnp.float32) + b.astype(jnp.float32)).astype(x.dtype)


def _rope_tables(pos, rot, theta):
    inv = 1.0 / (theta ** (jnp.arange(0, rot, 2, dtype=jnp.float32) / rot))
    ang = pos.astype(jnp.float32)[:, None] * inv[None, :]
    return jnp.cos(ang), jnp.sin(ang)


def _rope(x, cos, sin):
    half = cos.shape[-1]
    shape = (cos.shape[0],) + (1,) * (x.ndim - 3) + (half,)
    c = cos.reshape(shape).astype(x.dtype)
    s = sin.reshape(shape).astype(x.dtype)
    x1, x2, rest = x[..., :half], x[..., half:2 * half], x[..., 2 * half:]
    return jnp.concatenate([x1 * c - x2 * s, x2 * c + x1 * s, rest], axis=-1)


def _gather_pages(pool, l, pt):
    rows = pool[l, pt]
    return rows.reshape((pt.shape[0], pt.shape[1] * pool.shape[2]) + pool.shape[3:])


def _map_seq_groups(fn, *xs):
    b = xs[0].shape[0]
    g = math.gcd(b, SEQ_GROUP)
    grouped = tuple(x.reshape((b // g, g) + x.shape[1:]) for x in xs)
    out = lax.map(lambda a: fn(*a), grouped)
    return out.reshape((b,) + out.shape[2:])


def _sweep_query_blocks(fn, qs, q_pos):
    b, t = qs[0].shape[:2]
    bq = math.gcd(t, Q_BLOCK)
    nb = t // bq
    blocks = tuple(jnp.swapaxes(q.reshape((b, nb, bq) + q.shape[2:]), 0, 1) for q in qs)
    out = lax.map(lambda a: fn(a[0], a[1]), (blocks, q_pos.reshape(nb, bq)))
    return jnp.swapaxes(out, 0, 1).reshape((b, t) + out.shape[3:])


def _diff_attend(q, k, v, q_pos, k_pos, lam):
    s = jnp.einsum('bqhmd,bkmd->bhmqk', q, k).astype(jnp.float32) * (DA_HD ** -0.5)
    causal = k_pos[None, :] <= q_pos[:, None]
    p = jax.nn.softmax(jnp.where(causal, s, -jnp.inf), axis=-1)
    w = p[:, :, 0] - lam * p[:, :, 1]
    return jnp.einsum('bhqk,bkv->bqhv', w.astype(v.dtype), v)


def _mla_knope(ckv, w_uk, g_kn):
    return _rms(jnp.einsum('bkr,rhd->bkhd', ckv, w_uk), g_kn)


def _mla_attend(q_nope, q_rope, k_nope, k_rope, ckv, w_uv, q_pos, k_pos):
    s = (jnp.einsum('bqhd,bkhd->bhqk', q_nope, k_nope)
         + jnp.einsum('bqhd,bkd->bhqk', q_rope, k_rope)).astype(jnp.float32) * ((MLA_NOPE + MLA_ROPE) ** -0.5)
    causal = k_pos[None, :] <= q_pos[:, None]
    p = jax.nn.softmax(jnp.where(causal, s, -jnp.inf), axis=-1)
    o_lat = jnp.einsum('bhqk,bkr->bqhr', p.astype(ckv.dtype), ckv)
    return jnp.einsum('bqhr,rhd->bqhd', o_lat, w_uv)


def _clamped_swiglu(u):
    g, lin = u[..., :D_FF], u[..., D_FF:]
    g = jnp.minimum(g, SWIGLU_LIMIT)
    lin = jnp.clip(lin, -SWIGLU_LIMIT, SWIGLU_LIMIT)
    return g * jax.nn.sigmoid(SWIGLU_ALPHA * g) * (lin + 1.0)


def _moe(h, w_router, b_router, w_up, b_up, w_down, b_down):
    shape = h.shape
    xt = h.reshape(-1, D_MODEL)
    n = xt.shape[0]
    logits = (xt @ w_router).astype(jnp.float32) + b_router.astype(jnp.float32)
    top_v, top_e = lax.top_k(logits, TOP_K)
    gate = jax.nn.softmax(top_v, axis=-1)
    a = n * TOP_K
    e_flat = top_e.reshape(a)
    order = jnp.argsort(e_flat)
    e_sorted = e_flat[order]
    tok_sorted = (order // TOP_K).astype(jnp.int32)
    gate_sorted = gate.reshape(a)[order]
    counts = jnp.bincount(e_flat, length=N_EXPERTS)
    padded = (counts + MOE_BLOCK - 1) // MOE_BLOCK * MOE_BLOCK
    start = jnp.cumsum(counts) - counts
    pend = jnp.cumsum(padded)
    pstart = pend - padded
    dest = pstart[e_sorted] + jnp.arange(a) - start[e_sorted]
    n_blocks = -(-a // MOE_BLOCK) + N_EXPERTS
    slots = n_blocks * MOE_BLOCK
    slot_tok = jnp.zeros((slots,), jnp.int32).at[dest].set(tok_sorted)
    slot_w = jnp.zeros((slots,), jnp.float32).at[dest].set(gate_sorted)
    block_e = jnp.minimum(jnp.searchsorted(pend, jnp.arange(n_blocks) * MOE_BLOCK, side='right'), N_EXPERTS - 1)

    def expert_block(args):
        e, tok = args
        u = xt[tok] @ w_up[e] + b_up[e]
        return _clamped_swiglu(u) @ w_down[e] + b_down[e]

    yb = lax.map(expert_block, (block_e, slot_tok.reshape(n_blocks, MOE_BLOCK)))
    y = jnp.zeros((n, D_MODEL), jnp.float32).at[slot_tok].add(
        yb.reshape(slots, D_MODEL).astype(jnp.float32) * slot_w[:, None])
    return y.astype(h.dtype).reshape(shape)


def _token_mixer(h, pos, start, rope, l, P, past):
    b, t, _ = h.shape
    (cos_a, sin_a), (cos_m, sin_m) = rope
    k_pos = jnp.arange(start + t, dtype=jnp.int32)
    split_at = np.cumsum(IN_WIDTHS)[:-1].tolist()
    z_q, z_k, z_v, z_glu, z_qa, z_kva, z_kr, z_gate = jnp.split(h @ P['w_in'][l], split_at, axis=-1)

    q = _rope(_rms(z_q.reshape(b, t, DA_HEADS, 2, DA_HD), P['da_g_q'][l]), cos_a, sin_a)
    k = _rope(_rms(z_k.reshape(b, t, 2, DA_HD), P['da_g_k'][l]), cos_a, sin_a)
    v = z_v
    lp = P['da_lambda'][l].astype(jnp.float32)
    lam_init = 0.8 - 0.6 * math.exp(-0.3 * l)
    lam = jnp.exp(jnp.sum(lp[0] * lp[1])) - jnp.exp(jnp.sum(lp[2] * lp[3])) + lam_init
    if past is None:
        o_da = _sweep_query_blocks(lambda qb, pb: _diff_attend(qb[0], k, v, pb, pos, lam), (q,), pos)
    else:
        ck, cv, cckv, ckr, sconv, pt = past

        def da_group(q_g, k_g, v_g, pt_g):
            kk = jnp.concatenate([_gather_pages(ck, l, pt_g), k_g], axis=1)
            vv = jnp.concatenate([_gather_pages(cv, l, pt_g), v_g], axis=1)
            return _diff_attend(q_g, kk, vv, pos, k_pos, lam)

        o_da = _map_seq_groups(da_group, q, k, v, pt)
    o_da = (_rms(o_da, P['da_g_out'][l]) * (1.0 - lam_init)).reshape(b, t, DA_HEADS * DA_VD)

    u = z_glu[..., :CONV_C] * jax.nn.sigmoid(z_glu[..., CONV_C:])
    prev = jnp.zeros((b, CONV_W - 1, CONV_C), u.dtype) if past is None else sconv[l]
    ext = jnp.concatenate([prev, u], axis=1)
    y = lax.conv_general_dilated(ext, P['cv_w_dw'][l][:, None, :], (1,), 'VALID',
                                 dimension_numbers=('NWC', 'WIO', 'NWC'), feature_group_count=CONV_C)
    o_cv = jax.nn.silu(_layernorm(y + P['cv_b_dw'][l], P['cv_ln_g'][l], P['cv_ln_b'][l]))
    conv_state = ext[:, -(CONV_W - 1):]

    cq = _rms(z_qa, P['mla_g_qa'][l])
    qh = (cq @ P['mla_w_qb'][l]).reshape(b, t, MLA_HEADS, MLA_NOPE + MLA_ROPE)
    q_nope = _rms(qh[..., :MLA_NOPE], P['mla_g_qn'][l])
    q_rope = _rope(_rms(qh[..., MLA_NOPE:], P['mla_g_qr'][l]), cos_m, sin_m)
    ckv = _rms(z_kva, P['mla_g_kva'][l])
    krope = _rope(_rms(z_kr, P['mla_g_kr'][l]), cos_m, sin_m)
    w_uk, w_uv, g_kn = P['mla_w_uk'][l], P['mla_w_uv'][l], P['mla_g_kn'][l]
    if past is None:
        k_nope = _mla_knope(ckv, w_uk, g_kn)
        o_mla = _sweep_query_blocks(
            lambda qb, pb: _mla_attend(qb[0], qb[1], k_nope, krope, ckv, w_uv, pb, pos), (q_nope, q_rope), pos)
    else:
        def mla_group(qn_g, qr_g, ckv_g, kr_g, pt_g):
            c_all = jnp.concatenate([_gather_pages(cckv, l, pt_g), ckv_g], axis=1)
            r_all = jnp.concatenate([_gather_pages(ckr, l, pt_g), kr_g], axis=1)
            return _mla_attend(qn_g, qr_g, _mla_knope(c_all, w_uk, g_kn), r_all, c_all, w_uv, pos, k_pos)

        o_mla = _map_seq_groups(mla_group, q_nope, q_rope, ckv, krope, pt)
    o_mla = o_mla.reshape(b, t, MLA_HEADS * MLA_V)

    gates = jax.nn.sigmoid(z_gate.reshape(b, t, N_BRANCH, D_MODEL))
    branches = jnp.stack([o_da, o_cv, o_mla], axis=2)
    proj = jnp.einsum('btnc,ncd->btnd', branches, P['w_branch'][l])
    merged = jnp.einsum('btnd,btnd->btd', gates, proj)
    return merged @ P['w_out'][l], (k, v, ckv, krope, conv_state)


def _trunk(x, c, P, past):
    b, t, _ = x.shape
    start = 0 if past is None else past[-1].shape[1] * PAGE_SIZE
    pos = start + jnp.arange(t, dtype=jnp.int32)
    rope = (_rope_tables(pos, DA_ROT, DA_THETA), _rope_tables(pos, MLA_ROPE, MLA_THETA))
    cs = jax.nn.silu(c)
    new = []
    for l in range(DEPTH):
        mod = (cs @ P['w_ada'][l] + P['b_ada'][l])[:, None, :]
        sh1, sc1, g1, sh2, sc2, g2 = jnp.split(mod, 6, axis=-1)
        h = _rms(x, P['g_norm_mix'][l]) * (1.0 + sc1) + sh1
        mix, st = _token_mixer(h, pos, start, rope, l, P, past)
        x = x + g1 * mix
        h = _rms(x, P['g_norm_ffn'][l]) * (1.0 + sc2) + sh2
        x = x + g2 * _moe(h, P['moe_w_router'][l], P['moe_b_router'][l], P['moe_w_up'][l],
                          P['moe_b_up'][l], P['moe_w_down'][l], P['moe_b_down'][l])
        new.append(st)
    stacked = [jnp.stack(s, axis=0) for s in zip(*new)]
    return x, stacked


def setup_inputs(seed: int = 0) -> dict:
    keys = jax.random.split(jax.random.key(seed), 48)
    counter = [0]

    def nrm(shape, scale=1.0):
        kk = keys[counter[0]]
        counter[0] += 1
        return jax.random.normal(kk, shape, jnp.float32) * scale

    def gain(shape):
        return 1.0 + nrm(shape, 0.02)

    n_pages = PAST_LEN // PAGE_SIZE
    n_used = DEC_BATCH * n_pages
    n_phys = n_used + n_used // 4
    page_table = jax.random.permutation(keys[-1], n_phys)[:n_used].reshape(DEC_BATCH, n_pages).astype(jnp.int32)
    L, D = DEPTH, D_MODEL
    return {
        'x_prompt': nrm((BATCH, SEQ, D)),
        'x_sample': nrm((DEC_BATCH, DEC_SEQ, D)),
        'c_prompt': nrm((BATCH, D)),
        'c_sample': nrm((DEC_BATCH, D)),
        'cache_da_k': nrm((L, n_phys, PAGE_SIZE, 2, DA_HD)),
        'cache_da_v': nrm((L, n_phys, PAGE_SIZE, DA_VD)),
        'cache_mla_ckv': nrm((L, n_phys, PAGE_SIZE, MLA_KV_LORA)),
        'cache_mla_krope': nrm((L, n_phys, PAGE_SIZE, MLA_ROPE)),
        'state_conv': nrm((L, DEC_BATCH, CONV_W - 1, CONV_C), 0.5),
        'page_table': page_table,
        'w_ada': nrm((L, D, 6 * D), 0.5 * D ** -0.5),
        'b_ada': nrm((L, 6 * D), 0.02),
        'g_norm_mix': gain((L, D)),
        'g_norm_ffn': gain((L, D)),
        'w_in': nrm((L, D, N_IN), D ** -0.5),
        'da_g_q': gain((L, 2, DA_HD)),
        'da_g_k': gain((L, 2, DA_HD)),
        'da_lambda': nrm((L, 4, DA_HD), 0.1),
        'da_g_out': gain((L, DA_VD)),
        'cv_w_dw': nrm((L, CONV_W, CONV_C), CONV_W ** -0.5),
        'cv_b_dw': nrm((L, CONV_C), 0.02),
        'cv_ln_g': gain((L, CONV_C)),
        'cv_ln_b': nrm((L, CONV_C), 0.02),
        'mla_g_qa': gain((L, MLA_Q_LORA)),
        'mla_w_qb': nrm((L, MLA_Q_LORA, MLA_HEADS * (MLA_NOPE + MLA_ROPE)), MLA_Q_LORA ** -0.5),
        'mla_g_kva': gain((L, MLA_KV_LORA)),
        'mla_w_uk': nrm((L, MLA_KV_LORA, MLA_HEADS, MLA_NOPE), MLA_KV_LORA ** -0.5),
        'mla_w_uv': nrm((L, MLA_KV_LORA, MLA_HEADS, MLA_V), MLA_KV_LORA ** -0.5),
        'mla_g_qn': gain((L, MLA_NOPE)),
        'mla_g_kn': gain((L, MLA_NOPE)),
        'mla_g_qr': gain((L, MLA_ROPE)),
        'mla_g_kr': gain((L, MLA_ROPE)),
        'w_branch': nrm((L, N_BRANCH, BRANCH_W, D), BRANCH_W ** -0.5),
        'w_out': nrm((L, D, D), D ** -0.5),
        'moe_w_router': nrm((L, D, N_EXPERTS), D ** -0.5),
        'moe_b_router': nrm((L, N_EXPERTS), 0.01),
        'moe_w_up': nrm((L, N_EXPERTS, D, 2 * D_FF), D ** -0.5),
        'moe_b_up': nrm((L, N_EXPERTS, 2 * D_FF), 0.02),
        'moe_w_down': nrm((L, N_EXPERTS, D_FF, D), D_FF ** -0.5),
        'moe_b_down': nrm((L, N_EXPERTS, D), 0.02),
    }


def reference(x_prompt, x_sample, c_prompt, c_sample, cache_da_k, cache_da_v, cache_mla_ckv, cache_mla_krope,
              state_conv, page_table, w_ada, b_ada, g_norm_mix, g_norm_ffn, w_in, da_g_q, da_g_k, da_lambda,
              da_g_out, cv_w_dw, cv_b_dw, cv_ln_g, cv_ln_b, mla_g_qa, mla_w_qb, mla_g_kva, mla_w_uk, mla_w_uv,
              mla_g_qn, mla_g_kn, mla_g_qr, mla_g_kr, w_branch, w_out, moe_w_router, moe_b_router, moe_w_up,
              moe_b_up, moe_w_down, moe_b_down):
    P = dict(w_ada=w_ada, b_ada=b_ada, g_norm_mix=g_norm_mix, g_norm_ffn=g_norm_ffn, w_in=w_in,
             da_g_q=da_g_q, da_g_k=da_g_k, da_lambda=da_lambda, da_g_out=da_g_out,
             cv_w_dw=cv_w_dw, cv_b_dw=cv_b_dw, cv_ln_g=cv_ln_g, cv_ln_b=cv_ln_b,
             mla_g_qa=mla_g_qa, mla_w_qb=mla_w_qb, mla_g_kva=mla_g_kva, mla_w_uk=mla_w_uk, mla_w_uv=mla_w_uv,
             mla_g_qn=mla_g_qn, mla_g_kn=mla_g_kn, mla_g_qr=mla_g_qr, mla_g_kr=mla_g_kr,
             w_branch=w_branch, w_out=w_out, moe_w_router=moe_w_router, moe_b_router=moe_b_router,
             moe_w_up=moe_w_up, moe_b_up=moe_b_up, moe_w_down=moe_w_down, moe_b_down=moe_b_down)
    y_prompt, (pk, pv, pc, pr, ps) = _trunk(x_prompt, c_prompt, P, None)
    past = (cache_da_k, cache_da_v, cache_mla_ckv, cache_mla_krope, state_conv, page_table)
    y_sample, (sk, sv, sc, sr, ss) = _trunk(x_sample, c_sample, P, past)
    return (y_prompt, y_sample, pk, pv, pc, pr, ps, sk, sv, sc, sr, ss)
```

```python
import functools
import math

import numpy as np
import jax
import jax.numpy as jnp
from jax import lax
from jax.experimental import pallas as pl
from jax.experimental.pallas import tpu as pltpu

F32 = jnp.float32
BF16 = jnp.bfloat16

D_MODEL = 1024
DEPTH = 4
PAGE_SIZE = 128
DA_HEADS = 4
DA_HD = 64
DA_VD = 2 * DA_HD
DA_ROT = DA_HD // 4
DA_THETA = 500000.0
CONV_C = 512
CONV_W = 31
MLA_HEADS = 8
MLA_NOPE = 64
MLA_ROPE = 32
MLA_V = 64
MLA_Q_LORA = 384
MLA_KV_LORA = 256
MLA_THETA = 10000.0
N_BRANCH = 3
BRANCH_W = 512
N_EXPERTS = 32
TOP_K = 4
D_FF = D_MODEL
SWIGLU_LIMIT = 7.0
SWIGLU_ALPHA = 1.702
NORM_EPS = 1e-6
IN_WIDTHS = (DA_HEADS * 2 * DA_HD, 2 * DA_HD, DA_VD, 2 * CONV_C, MLA_Q_LORA, MLA_KV_LORA, MLA_ROPE,
             N_BRANCH * D_MODEL)

LANES = 128
VMEM_LIMIT = 56 * 1024 * 1024
NEG = -1e30

_OFF_Q, _OFF_K, _OFF_V, _OFF_GLU = 0, 512, 640, 768
_OFF_QA, _OFF_KVA, _OFF_KR, _OFF_GATE = 1792, 2176, 2432, 2560
_N_IN_PAD = 2560 + N_BRANCH * D_MODEL


def _cp(sem, vmem=VMEM_LIMIT):
    return pltpu.CompilerParams(dimension_semantics=sem, vmem_limit_bytes=vmem)


def _dot(a, b):
    return jnp.dot(a.astype(BF16), b.astype(BF16), preferred_element_type=F32)


def _dot_nt(a, b):
    return lax.dot_general(a.astype(BF16), b.astype(BF16), (((1,), (1,)), ((), ())),
                           preferred_element_type=F32)


def _dot_tn(a, b):
    return lax.dot_general(a.astype(BF16), b.astype(BF16), (((0,), (0,)), ((), ())),
                           preferred_element_type=F32)


def _lane(shape):
    return lax.broadcasted_iota(jnp.int32, shape, len(shape) - 1)


def _rope_lanes(y, c, sa, sb, half):
    w = y.shape[-1]
    return y * c + pltpu.roll(y, w - half, 1) * sa + pltpu.roll(y, half, 1) * sb


def _ada_kernel(c_ref, w_ref, b_ref, o_ref):
    c = c_ref[...]
    cs = c * jax.nn.sigmoid(c)
    o_ref[0] = _dot(cs, w_ref[0]) + b_ref[0]


def _ada(c, w_ada, b_ada):
    r = c.shape[0]
    tn = 1536
    n = w_ada.shape[-1]
    return pl.pallas_call(
        _ada_kernel,
        grid=(DEPTH, n // tn),
        in_specs=[pl.BlockSpec((r, D_MODEL), lambda l, j: (0, 0)),
                  pl.BlockSpec((1, D_MODEL, tn), lambda l, j: (l, 0, j)),
                  pl.BlockSpec((1, 1, tn), lambda l, j: (l, 0, j))],
        out_specs=pl.BlockSpec((1, r, tn), lambda l, j: (l, 0, j)),
        out_shape=jax.ShapeDtypeStruct((DEPTH, r, n), F32),
        compiler_params=_cp(("parallel", "parallel")),
        name="ada",
    )(c, w_ada, b_ada.reshape(DEPTH, 1, n))


def _token_prep_kernel(with_kv, x_ref, sc_ref, sh_ref, gmix_ref, dac_ref, dasa_ref, dasb_ref,
                       mlc_ref, mlsa_ref, mlsb_ref, w_ref, gq_ref, gk_ref, gqa_ref, gkva_ref, gkr_ref,
                       gqm_ref, gkn_ref, wqb_ref, wuk_ref, wuv_ref,
                       qda_ref, k_ref, v_ref, u_ref, qmla_ref, ckv_ref, kr_ref, gate_ref, *kv_refs):
    x = x_ref[...]
    xn = x * lax.rsqrt(jnp.mean(x * x, axis=-1, keepdims=True) + NORM_EPS) * gmix_ref[...]
    h = (xn * (1.0 + sc_ref[0]) + sh_ref[0]).astype(BF16)
    tm = x.shape[0]
    lane = _lane((tm, LANES))
    lo = lane < DA_HD

    def half_rms(z, g):
        zz = z * z
        s_lo = jnp.sum(jnp.where(lo, zz, 0.0), axis=-1, keepdims=True)
        s_hi = jnp.sum(jnp.where(lo, 0.0, zz), axis=-1, keepdims=True)
        r = jnp.where(lo, lax.rsqrt(s_lo / DA_HD + NORM_EPS), lax.rsqrt(s_hi / DA_HD + NORM_EPS))
        return z * r * g

    zq = jnp.dot(h, w_ref[:, _OFF_Q:_OFF_Q + 512], preferred_element_type=F32)
    gq = gq_ref[...]
    qs = []
    for hh in range(DA_HEADS):
        sl = slice(hh * LANES, (hh + 1) * LANES)
        y = half_rms(zq[:, sl], gq)
        y = _rope_lanes(y, dac_ref[0], dasa_ref[0], dasb_ref[0], DA_ROT // 2)
        qs.append((y * (DA_HD ** -0.5)).astype(BF16))
    qda_ref[...] = jnp.concatenate(qs, axis=-1)
    zk = jnp.dot(h, w_ref[:, _OFF_K:_OFF_K + 128], preferred_element_type=F32)
    k_ref[...] = _rope_lanes(half_rms(zk, gk_ref[...]), dac_ref[0], dasa_ref[0], dasb_ref[0], DA_ROT // 2)
    v_ref[...] = jnp.dot(h, w_ref[:, _OFF_V:_OFF_V + 128], preferred_element_type=F32)

    zg = jnp.dot(h, w_ref[:, _OFF_GLU:_OFF_GLU + 1024], preferred_element_type=F32)
    u_ref[...] = zg[:, :CONV_C] * jax.nn.sigmoid(zg[:, CONV_C:])

    zqa = jnp.dot(h, w_ref[:, _OFF_QA:_OFF_QA + MLA_Q_LORA], preferred_element_type=F32)
    cq = zqa * lax.rsqrt(jnp.mean(zqa * zqa, axis=-1, keepdims=True) + NORM_EPS) * gqa_ref[...]
    qh = _dot(cq, wqb_ref[...])
    nope = lane < MLA_NOPE
    ropem = jnp.logical_and(lane >= MLA_NOPE, lane < MLA_NOPE + MLA_ROPE)
    gqm = gqm_ref[...]
    mlc, mlsa, mlsb = mlc_ref[0], mlsa_ref[0], mlsb_ref[0]
    qm = []
    for hh in range(MLA_HEADS):
        z = qh[:, hh * LANES:(hh + 1) * LANES]
        zz = z * z
        s_n = jnp.sum(jnp.where(nope, zz, 0.0), axis=-1, keepdims=True)
        s_r = jnp.sum(jnp.where(ropem, zz, 0.0), axis=-1, keepdims=True)
        r = jnp.where(nope, lax.rsqrt(s_n / MLA_NOPE + NORM_EPS), lax.rsqrt(s_r / MLA_ROPE + NORM_EPS))
        y = _rope_lanes(z * r * gqm, mlc, mlsa, mlsb, MLA_ROPE // 2)
        qm.append((y * ((MLA_NOPE + MLA_ROPE) ** -0.5)).astype(BF16))
    qmla_ref[...] = jnp.concatenate(qm, axis=-1)

    zkva = jnp.dot(h, w_ref[:, _OFF_KVA:_OFF_KVA + MLA_KV_LORA], preferred_element_type=F32)
    ckv = zkva * lax.rsqrt(jnp.mean(zkva * zkva, axis=-1, keepdims=True) + NORM_EPS) * gkva_ref[...]
    ckv_ref[...] = ckv
    zkr = jnp.dot(h, w_ref[:, _OFF_KR:_OFF_KR + 128], preferred_element_type=F32)
    s_kr = jnp.sum(zkr * zkr, axis=-1, keepdims=True)
    kr = _rope_lanes(zkr * lax.rsqrt(s_kr / MLA_ROPE + NORM_EPS) * gkr_ref[...], mlc, mlsa, mlsb,
                     MLA_ROPE // 2)
    kr_ref[...] = kr[:, MLA_NOPE:MLA_NOPE + MLA_ROPE]

    gate_ref[...] = jax.nn.sigmoid(
        jnp.dot(h, w_ref[:, _OFF_GATE:_OFF_GATE + N_BRANCH * D_MODEL], preferred_element_type=F32)
    ).astype(BF16)

    if with_kv:
        kcat_ref, vh_ref, kb_ref, vb_ref = kv_refs
        kb_ref[...] = k_ref[...].astype(BF16)
        vb_ref[...] = v_ref[...].astype(BF16)
        kn = _dot(ckv, wuk_ref[...])
        gkn = gkn_ref[...]
        ks = []
        for hh in range(MLA_HEADS):
            z = kn[:, hh * LANES:(hh + 1) * LANES]
            s = jnp.sum(z * z, axis=-1, keepdims=True)
            ks.append((z * lax.rsqrt(s / MLA_NOPE + NORM_EPS) * gkn + kr).astype(BF16))
        kcat_ref[...] = jnp.concatenate(ks, axis=-1)
        vh_ref[...] = _dot(ckv, wuv_ref[...]).astype(BF16)


def _token_prep(x, sc, sh, tabs, lw, nt, nb, tm, with_kv):
    n = x.shape[0]
    r = sc.shape[1]
    row = lambda t, b: (b * nt + t, 0)
    const2 = lambda t, b: (0, 0)
    tab_spec = pl.BlockSpec((1, tm, LANES), lambda t, b: (t, 0, 0))
    mod_spec = pl.BlockSpec((1, r, D_MODEL), lambda t, b: (b, 0, 0))

    def full(a):
        return pl.BlockSpec(a.shape, const2)

    def out(w, dt):
        return jax.ShapeDtypeStruct((n, w), dt), pl.BlockSpec((tm, w), row)

    outs = [out(512, BF16), out(128, F32), out(128, F32), out(CONV_C, F32), out(1024, BF16),
            out(MLA_KV_LORA, F32), out(MLA_ROPE, F32), out(N_BRANCH * D_MODEL, BF16)]
    if with_kv:
        outs += [out(1024, BF16), out(1024, BF16), out(128, BF16), out(128, BF16)]
    small = [lw["g_q"], lw["g_k"], lw["g_qa"], lw["g_kva"], lw["g_kr"], lw["g_qm"], lw["g_kn"],
             lw["w_qb"], lw["w_uk"], lw["w_uv"]]
    return pl.pallas_call(
        functools.partial(_token_prep_kernel, with_kv),
        grid=(nt, nb),
        in_specs=[pl.BlockSpec((tm, D_MODEL), row), mod_spec, mod_spec, full(lw["g_mix"])]
                 + [tab_spec] * 6 + [full(lw["w_in"])] + [full(a) for a in small],
        out_specs=[o[1] for o in outs],
        out_shape=[o[0] for o in outs],
        compiler_params=_cp(("parallel", "parallel")),
        name="token_prep",
    )(x, sc, sh, lw["g_mix"], *tabs, lw["w_in"], *small)


def _flash_kernel(diff, tk, lam_ref, q_ref, k_ref, v_ref, o_ref):
    qi = pl.program_id(2)
    q = q_ref[...]
    tq = q.shape[0]
    if diff:
        lo = _lane(q.shape) < DA_HD
        zero = jnp.zeros_like(q)
        streams = [jnp.where(lo, q, zero), jnp.where(lo, zero, q)]
    else:
        streams = [q]
    q_pos = qi * tq + lax.broadcasted_iota(jnp.int32, (tq, tk), 0)
    col = lax.broadcasted_iota(jnp.int32, (tq, tk), 1)
    dv = v_ref.shape[-1]

    def body(j, carry):
        kt = k_ref[pl.ds(pl.multiple_of(j * tk, tk), tk), :]
        vt = v_ref[pl.ds(pl.multiple_of(j * tk, tk), tk), :]
        visible = (j * tk + col) <= q_pos
        new = []
        for qs, (m, l, acc) in zip(streams, carry):
            s = jnp.where(visible, _dot_nt(qs, kt), NEG)
            m2 = jnp.maximum(m, jnp.max(s, axis=-1, keepdims=True))
            a = jnp.exp(m - m2)
            p = jnp.exp(s - m2)
            new.append((m2, a * l + jnp.sum(p, axis=-1, keepdims=True), a * acc + _dot(p, vt)))
        return tuple(new)

    init = tuple((jnp.full((tq, 1), NEG, F32), jnp.zeros((tq, 1), F32), jnp.zeros((tq, dv), F32))
                 for _ in streams)
    res = lax.fori_loop(0, ((qi + 1) * tq + tk - 1) // tk, body, init)
    outs = [acc / l for (_, l, acc) in res]
    if diff:
        o_ref[...] = outs[0] - lam_ref[0] * outs[1]
    else:
        o_ref[...] = outs[0].astype(o_ref.dtype)


def _flash(lam, q, k, v, nb, t, heads, shared_kv, diff, out_dtype):
    tq = min(512, t)
    tk = min(256, t)
    nq = t // tq
    kv_map = (lambda b, h, i, lam: (b, 0)) if shared_kv else (lambda b, h, i, lam: (b, h))
    return pl.pallas_call(
        functools.partial(_flash_kernel, diff, tk),
        grid_spec=pltpu.PrefetchScalarGridSpec(
            num_scalar_prefetch=1,
            grid=(nb, heads, nq),
            in_specs=[pl.BlockSpec((tq, LANES), lambda b, h, i, lam: (b * nq + i, h)),
                      pl.BlockSpec((t, LANES), kv_map),
                      pl.BlockSpec((t, LANES), kv_map)],
            out_specs=pl.BlockSpec((tq, LANES), lambda b, h, i, lam: (b * nq + i, h)),
        ),
        out_shape=jax.ShapeDtypeStruct((nb * t, heads * LANES), out_dtype),
        compiler_params=_cp(("parallel", "parallel", "arbitrary")),
        name="flash_da" if diff else "flash_mla",
    )(lam, q, k, v)


def _page_copies(pt_ref, seq, slot, n_pages, pools, bufs, sems, layer):
    def per_page(p, fn):
        page = pt_ref[seq, p]
        for pool, buf, sem in zip(pools, bufs, sems):
            fn(pltpu.make_async_copy(pool.at[layer, page],
                                     buf.at[slot, pl.ds(pl.multiple_of(p * PAGE_SIZE, PAGE_SIZE), PAGE_SIZE)],
                                     sem.at[slot]))

    return per_page


def _fetch_pages(pt_ref, seq, slot, n_pages, pools, bufs, sems, layer):
    per_page = _page_copies(pt_ref, seq, slot, n_pages, pools, bufs, sems, layer)

    def body(p, c):
        per_page(p, lambda cp: cp.start())
        return c

    lax.fori_loop(0, n_pages, body, 0)


def _wait_pages(pt_ref, seq, slot, n_pages, pools, bufs, sems, layer):
    per_page = _page_copies(pt_ref, seq, slot, n_pages, pools, bufs, sems, layer)

    def body(p, c):
        per_page(p, lambda cp: cp.wait())
        return c

    lax.fori_loop(0, n_pages, body, 0)


def _paged_prologue(pt_ref, n_pages, pools, bufs, sems, layer):
    b = pl.program_id(0)
    nb = pl.num_programs(0)
    slot = b % 2

    @pl.when(b == 0)
    def _():
        _fetch_pages(pt_ref, 0, 0, n_pages, pools, bufs, sems, layer)

    @pl.when(b + 1 < nb)
    def _():
        _fetch_pages(pt_ref, b + 1, 1 - slot, n_pages, pools, bufs, sems, layer)

    _wait_pages(pt_ref, b, slot, n_pages, pools, bufs, sems, layer)
    return slot


def _col_softmax_stats(s_ref, n_chunks, chunk, s_new):
    def mx(c, m):
        return jnp.maximum(m, jnp.max(s_ref[pl.ds(pl.multiple_of(c * chunk, chunk), chunk), :], axis=0,
                                      keepdims=True))

    return lax.fori_loop(0, n_chunks, mx, jnp.max(s_new, axis=0, keepdims=True))


def _paged_da_kernel(layer, n_pages, chunk, dec, pt_ref, lam_ref, q_ref, kn_ref, vn_ref, kpool, vpool,
                     o_ref, kbuf, vbuf, s_ref, ksem, vsem):
    slot = _paged_prologue(pt_ref, n_pages, (kpool, vpool), (kbuf, vbuf), (ksem, vsem), layer)
    tk = n_pages * PAGE_SIZE
    n_chunks = tk // chunk
    lam = lam_ref[0]

    q = q_ref[...].astype(F32)
    lane = _lane((dec, LANES))
    zero = jnp.zeros((dec, LANES), F32)
    pad = jnp.zeros((64 - DA_HEADS * dec, LANES), F32)
    rows = []
    for mp in range(2):
        keep = (lane < DA_HD) if mp == 0 else (lane >= DA_HD)
        rows += [jnp.where(keep, q[:, hh * LANES:(hh + 1) * LANES], zero) for hh in range(DA_HEADS)]
        rows.append(pad)
    qx = jnp.concatenate(rows, axis=0).astype(BF16)

    def score(c, carry):
        off = pl.multiple_of(c * chunk, chunk)
        s_ref[pl.ds(off, chunk), :] = _dot_nt(kbuf[slot, pl.ds(off, chunk), :], qx)
        return carry

    lax.fori_loop(0, n_chunks, score, 0)
    s_new = _dot_nt(kn_ref[...], qx)
    jrow = lax.broadcasted_iota(jnp.int32, (dec, LANES), 0)
    icol = _lane((dec, LANES)) % dec
    s_new = jnp.where(jrow <= icol, s_new, NEG)
    m = _col_softmax_stats(s_ref, n_chunks, chunk, s_new)
    p_new = jnp.exp(s_new - m)

    def accum(c, carry):
        l, acc = carry
        off = pl.multiple_of(c * chunk, chunk)
        p = jnp.exp(s_ref[pl.ds(off, chunk), :] - m)
        s_ref[pl.ds(off, chunk), :] = p
        return l + jnp.sum(p, axis=0, keepdims=True), acc

    l, _ = lax.fori_loop(0, n_chunks, accum, (jnp.sum(p_new, axis=0, keepdims=True), 0))
    inv = 1.0 / l
    def combine(p):
        pn = p * inv
        return pn - lam * pltpu.roll(pn, 64, 1)

    def pv(c, acc):
        off = pl.multiple_of(c * chunk, chunk)
        w = combine(s_ref[pl.ds(off, chunk), :])
        return acc + _dot_tn(w, vbuf[slot, pl.ds(off, chunk), :])

    acc = lax.fori_loop(0, n_chunks, pv, _dot_tn(combine(p_new), vn_ref[...]))
    o_ref[...] = jnp.concatenate([acc[hh * dec:(hh + 1) * dec, :] for hh in range(DA_HEADS)], axis=-1)


def _paged_da(layer, pt, lam, q, k_new, v_new, kpool, vpool, dec):
    ns, n_pages = pt.shape
    tk = n_pages * PAGE_SIZE
    chunk = min(2048, tk)
    seq = lambda b, pt, lam: (b, 0)
    return pl.pallas_call(
        functools.partial(_paged_da_kernel, layer, n_pages, chunk, dec),
        grid_spec=pltpu.PrefetchScalarGridSpec(
            num_scalar_prefetch=2,
            grid=(ns,),
            in_specs=[pl.BlockSpec((dec, 512), seq), pl.BlockSpec((dec, LANES), seq),
                      pl.BlockSpec((dec, LANES), seq),
                      pl.BlockSpec(memory_space=pl.ANY), pl.BlockSpec(memory_space=pl.ANY)],
            out_specs=pl.BlockSpec((dec, 512), seq),
            scratch_shapes=[pltpu.VMEM((2, tk, LANES), F32), pltpu.VMEM((2, tk, LANES), F32),
                            pltpu.VMEM((tk, LANES), F32),
                            pltpu.SemaphoreType.DMA((2,)), pltpu.SemaphoreType.DMA((2,))],
        ),
        out_shape=jax.ShapeDtypeStruct((ns * dec, 512), F32),
        compiler_params=_cp(("arbitrary",)),
        name="paged_da",
    )(pt, lam, q, k_new, v_new, kpool, vpool)


def _paged_mla_kernel(layer, n_pages, chunk, dec, pt_ref, q_ref, cn_ref, rn_ref, wuk_ref, wuv_ref, gkn_ref,
                      seln_ref, selr_ref, fold_ref, expand_ref, cpool, rpool,
                      o_ref, cbuf, rbuf, s_ref, csem, rsem):
    slot = _paged_prologue(pt_ref, n_pages, (cpool, rpool), (cbuf, rbuf), (csem, rsem), layer)
    tk = n_pages * PAGE_SIZE
    n_chunks = tk // chunk
    nrow = MLA_HEADS * dec

    q = q_ref[...]
    qn = jnp.dot(q, seln_ref[...], preferred_element_type=F32) * gkn_ref[...]
    qr = jnp.dot(q, selr_ref[...], preferred_element_type=F32)
    reps = LANES // dec
    rowh = lax.broadcasted_iota(jnp.int32, (LANES, 1), 0) // dec
    qn_t = jnp.concatenate([qn] * reps, axis=0)
    qx = jnp.where(_lane(qn_t.shape) // MLA_NOPE == rowh, qn_t, 0.0).astype(BF16)
    qr_t = jnp.concatenate([qr] * reps, axis=0)
    qr_m = jnp.where(_lane(qr_t.shape) // MLA_ROPE == rowh, qr_t, 0.0).astype(BF16)
    qrx = jnp.dot(qr_m, fold_ref[...], preferred_element_type=F32).astype(BF16)
    wuk = wuk_ref[...]
    expand = expand_ref[...]

    def scores(c_lat, c_rope):
        kn = _dot(c_lat, wuk)
        kk = kn * kn
        hi = kk.astype(BF16)
        ssq = (jnp.dot(hi, expand, preferred_element_type=F32)
               + jnp.dot((kk - hi.astype(F32)).astype(BF16), expand, preferred_element_type=F32))
        return _dot_nt(kn, qx) * lax.rsqrt(ssq / MLA_NOPE + NORM_EPS) + _dot_nt(c_rope, qrx)

    def score(c, carry):
        off = pl.multiple_of(c * chunk, chunk)
        s_ref[pl.ds(off, chunk), :] = scores(cbuf[slot, pl.ds(off, chunk), :], rbuf[slot, pl.ds(off, chunk), :])
        return carry

    lax.fori_loop(0, n_chunks, score, 0)
    c_new = cn_ref[...]
    s_new = scores(c_new, rn_ref[...])
    jrow = lax.broadcasted_iota(jnp.int32, (dec, LANES), 0)
    icol = _lane((dec, LANES)) % dec
    s_new = jnp.where(jrow <= icol, s_new, NEG)
    m = _col_softmax_stats(s_ref, n_chunks, chunk, s_new)
    p_new = jnp.exp(s_new - m)

    def pv(c, carry):
        l, acc = carry
        off = pl.multiple_of(c * chunk, chunk)
        p = jnp.exp(s_ref[pl.ds(off, chunk), :] - m)
        return (l + jnp.sum(p, axis=0, keepdims=True),
                acc + _dot_tn(p, cbuf[slot, pl.ds(off, chunk), :]))

    l, acc = lax.fori_loop(0, n_chunks, pv,
                           (jnp.sum(p_new, axis=0, keepdims=True), _dot_tn(p_new, c_new)))
    eye = (lax.broadcasted_iota(jnp.int32, (LANES, LANES), 0) == _lane((LANES, LANES))).astype(F32)
    l_col = jnp.sum(eye * l, axis=-1, keepdims=True)
    o_lat = acc / l_col
    wuv = wuv_ref[...]
    outs = []
    for hh in range(MLA_HEADS):
        outs.append(_dot(o_lat[hh * dec:(hh + 1) * dec, :], wuv[:, hh * MLA_V:(hh + 1) * MLA_V]))
    o_ref[...] = jnp.concatenate(outs, axis=-1).astype(o_ref.dtype)
    del nrow


def _paged_mla(layer, pt, q, c_new, r_new, lw, consts, cpool, rpool, dec):
    ns, n_pages = pt.shape
    tk = n_pages * PAGE_SIZE
    chunk = min(1024, tk)
    seq = lambda b, pt: (b, 0)
    const2 = lambda b, pt: (0, 0)
    small = [lw["w_uk_c"], lw["w_uv_c"], lw["g_kn_t"], consts["sel_nope"], consts["sel_rope"],
             consts["fold_rope"], consts["expand_head"]]
    return pl.pallas_call(
        functools.partial(_paged_mla_kernel, layer, n_pages, chunk, dec),
        grid_spec=pltpu.PrefetchScalarGridSpec(
            num_scalar_prefetch=1,
            grid=(ns,),
            in_specs=[pl.BlockSpec((dec, 1024), seq), pl.BlockSpec((dec, MLA_KV_LORA), seq),
                      pl.BlockSpec((dec, MLA_ROPE), seq)]
                     + [pl.BlockSpec(a.shape, const2) for a in small]
                     + [pl.BlockSpec(memory_space=pl.ANY), pl.BlockSpec(memory_space=pl.ANY)],
            out_specs=pl.BlockSpec((dec, MLA_HEADS * MLA_V), seq),
            scratch_shapes=[pltpu.VMEM((2, tk, MLA_KV_LORA), F32), pltpu.VMEM((2, tk, MLA_ROPE), F32),
                            pltpu.VMEM((tk, LANES), F32),
                            pltpu.SemaphoreType.DMA((2,)), pltpu.SemaphoreType.DMA((2,))],
        ),
        out_shape=jax.ShapeDtypeStruct((ns * dec, MLA_HEADS * MLA_V), BF16),
        compiler_params=_cp(("arbitrary",)),
        name="paged_mla",
    )(pt, q, c_new, r_new, *small, cpool, rpool)


def _conv_kernel(t, tt, ext_ref, w_ref, b_ref, g_ref, bb_ref, o_ref):
    w = w_ref[...]

    def tile(i, carry):
        base = pl.multiple_of(i * tt, tt)
        win = ext_ref[0, pl.ds(base, tt + CONV_W - 1), :]
        acc = jnp.zeros((tt, CONV_C), F32)
        for j in range(CONV_W):
            acc = acc + win[j:j + tt, :] * w[j:j + 1, :]
        y = acc + b_ref[...]
        mu = jnp.mean(y, axis=-1, keepdims=True)
        yc = y - mu
        yn = yc * lax.rsqrt(jnp.mean(yc * yc, axis=-1, keepdims=True) + NORM_EPS) * g_ref[...] + bb_ref[...]
        o_ref[0, pl.ds(base, tt), :] = (yn * jax.nn.sigmoid(yn)).astype(o_ref.dtype)
        return carry

    lax.fori_loop(0, t // tt, tile, 0)


def _conv(ext, lw, t):
    nb = ext.shape[0]
    tt = min(32, t)
    vec = pl.BlockSpec((1, CONV_C), lambda b: (0, 0))
    return pl.pallas_call(
        functools.partial(_conv_kernel, t, tt),
        grid=(nb,),
        in_specs=[pl.BlockSpec((1, CONV_W - 1 + t, CONV_C), lambda b: (b, 0, 0)),
                  pl.BlockSpec((CONV_W, CONV_C), lambda b: (0, 0)), vec, vec, vec],
        out_specs=pl.BlockSpec((1, t, CONV_C), lambda b: (b, 0, 0)),
        out_shape=jax.ShapeDtypeStruct((nb, t, CONV_C), BF16),
        compiler_params=_cp(("parallel",)),
        name="conv",
    )(ext, lw["cv_w"], lw["cv_b"], lw["cv_g"], lw["cv_bb"])


def _merge_route_kernel(lam_scale, x_ref, oda_ref, ocv_ref, omla_ref, gate_ref, g1_ref, sc2_ref, sh2_ref,
                        gout_ref, gffn_ref, wb0_ref, wb1_ref, wb2_ref, wout_ref, wr_ref, br_ref, tri_ref,
                        xo_ref, h_ref, mi_ref, mf_ref, cnt_ref, carry_ref):
    first = jnp.logical_and(pl.program_id(0) == 0, pl.program_id(1) == 0)

    @pl.when(first)
    def _():
        carry_ref[...] = jnp.zeros_like(carry_ref)

    oda = oda_ref[...]
    gout = gout_ref[...]
    das = []
    for hh in range(DA_HEADS):
        z = oda[:, hh * LANES:(hh + 1) * LANES]
        r = lax.rsqrt(jnp.mean(z * z, axis=-1, keepdims=True) + NORM_EPS)
        das.append((z * r * gout * lam_scale).astype(BF16))
    o_da = jnp.concatenate(das, axis=-1)
    gate = gate_ref[...].astype(F32)
    merged = (gate[:, :D_MODEL] * jnp.dot(o_da, wb0_ref[...], preferred_element_type=F32)
              + gate[:, D_MODEL:2 * D_MODEL] * jnp.dot(ocv_ref[...], wb1_ref[...], preferred_element_type=F32)
              + gate[:, 2 * D_MODEL:] * jnp.dot(omla_ref[...], wb2_ref[...], preferred_element_type=F32))
    x = x_ref[...] + g1_ref[0] * _dot(merged, wout_ref[...])
    xo_ref[...] = x
    xn = x * lax.rsqrt(jnp.mean(x * x, axis=-1, keepdims=True) + NORM_EPS) * gffn_ref[...]
    h = xn * (1.0 + sc2_ref[0]) + sh2_ref[0]
    h_ref[...] = h

    tm = x.shape[0]
    logits = _dot(h, wr_ref[...]) + br_ref[...]
    lane = _lane((tm, LANES))
    work = logits
    sel = jnp.zeros((tm, LANES), F32)
    vals, idxs = [], []
    for _ in range(TOP_K):
        mx = jnp.max(work, axis=-1, keepdims=True)
        idx = jnp.min(jnp.where(work == mx, lane, LANES), axis=-1, keepdims=True)
        hit = lane == idx
        sel = jnp.where(hit, 1.0, sel)
        work = jnp.where(hit, NEG * 2.0, work)
        vals.append(mx)
        idxs.append(idx)
    es = [jnp.exp(v - vals[0]) for v in vals]
    den = es[0] + es[1] + es[2] + es[3]
    rank_all = carry_ref[...] + jnp.dot(tri_ref[...], sel.astype(BF16), preferred_element_type=F32)
    carry_ref[...] = carry_ref[...] + jnp.sum(sel, axis=0, keepdims=True)
    mi = jnp.zeros((tm, LANES), jnp.int32)
    mf = jnp.zeros((tm, LANES), F32)
    for kk in range(TOP_K):
        rk = jnp.sum(jnp.where(lane == idxs[kk], rank_all, 0.0), axis=-1, keepdims=True).astype(jnp.int32)
        mi = jnp.where(lane == kk, idxs[kk], mi)
        mi = jnp.where(lane == TOP_K + kk, rk, mi)
        mf = jnp.where(lane == kk, es[kk] / den, mf)
    mi_ref[...] = mi
    mf_ref[...] = mf
    cnt_ref[...] = carry_ref[...]


def _merge_route(x, oda, ocv, omla, gate, g1, sc2, sh2, lw, lam_scale, nt, nb, tm):
    n = x.shape[0]
    r = g1.shape[1]
    row = lambda t, b: (b * nt + t, 0)
    const2 = lambda t, b: (0, 0)
    mod_spec = pl.BlockSpec((1, r, D_MODEL), lambda t, b: (b, 0, 0))
    tri = (np.arange(tm)[:, None] > np.arange(tm)[None, :]).astype(np.float32)
    tri = jnp.asarray(tri, BF16)
    small = [lw["g_out"], lw["g_ffn"], lw["w_b0"], lw["w_b1"], lw["w_b2"], lw["w_out"], lw["w_router"],
             lw["b_router"], tri]

    def rows(w):
        return pl.BlockSpec((tm, w), row)

    return pl.pallas_call(
        functools.partial(_merge_route_kernel, lam_scale),
        grid=(nt, nb),
        in_specs=[rows(D_MODEL), rows(512), rows(CONV_C), rows(omla.shape[1]), rows(N_BRANCH * D_MODEL),
                  mod_spec, mod_spec, mod_spec] + [pl.BlockSpec(a.shape, const2) for a in small],
        out_specs=[rows(D_MODEL), rows(D_MODEL), rows(LANES), rows(LANES),
                   pl.BlockSpec((1, LANES), const2)],
        out_shape=[jax.ShapeDtypeStruct((n, D_MODEL), F32), jax.ShapeDtypeStruct((n, D_MODEL), F32),
                   jax.ShapeDtypeStruct((n, LANES), jnp.int32), jax.ShapeDtypeStruct((n, LANES), F32),
                   jax.ShapeDtypeStruct((1, LANES), F32)],
        scratch_shapes=[pltpu.VMEM((1, LANES), F32)],
        compiler_params=_cp(("arbitrary", "arbitrary")),
        name="merge_route",
    )(x, oda, ocv, omla, gate, g1, sc2, sh2, *small)


def _row_copy(src, row, buf, slot, r, sem):
    return pltpu.make_async_copy(src.at[pl.ds(row, 1)], buf.at[slot, pl.ds(r, 1)], sem.at[slot])


def _experts_kernel(bm, be_ref, nblk_ref, tok0_ref, tokn_ref, h_hbm, wup_ref, bup_ref, wdn_ref, bdn_ref,
                    y_ref, xbuf, sem):
    i = pl.program_id(0)
    nblk = nblk_ref[0]
    slot = i % 2

    def fetch(tok_ref, sl):
        def body(r, c):
            _row_copy(h_hbm, tok_ref[0, 0, r], xbuf, sl, r, sem).start()
            return c

        lax.fori_loop(0, bm, body, 0, unroll=8)

    @pl.when(jnp.logical_and(i == 0, nblk > 0))
    def _():
        fetch(tok0_ref, 0)

    @pl.when(i + 1 < nblk)
    def _():
        fetch(tokn_ref, 1 - slot)

    @pl.when(i < nblk)
    def _():
        def wait_row(r, c):
            _row_copy(h_hbm, 0, xbuf, slot, r, sem).wait()
            return c

        lax.fori_loop(0, bm, wait_row, 0, unroll=8)
        xg = xbuf[slot].astype(BF16)
        u = jnp.dot(xg, wup_ref[0], preferred_element_type=F32) + bup_ref[0]
        g = jnp.minimum(u[:, :D_FF], SWIGLU_LIMIT)
        lin = jnp.clip(u[:, D_FF:], -SWIGLU_LIMIT, SWIGLU_LIMIT)
        act = g * jax.nn.sigmoid(SWIGLU_ALPHA * g) * (lin + 1.0)
        y_ref[...] = _dot(act, wdn_ref[0]) + bdn_ref[0]

    @pl.when(i >= nblk)
    def _():
        y_ref[...] = jnp.zeros_like(y_ref)


def _experts(block_e, nblk, slot_tok, h, lw, bm, n_blocks):
    def wmap(i, be, nb):
        return (be[i], 0, 0)

    tok_shape = (1, 1, bm)
    return pl.pallas_call(
        functools.partial(_experts_kernel, bm),
        grid_spec=pltpu.PrefetchScalarGridSpec(
            num_scalar_prefetch=2,
            grid=(n_blocks,),
            in_specs=[pl.BlockSpec(tok_shape, lambda i, be, nb: (0, 0, 0), memory_space=pltpu.SMEM),
                      pl.BlockSpec(tok_shape, lambda i, be, nb: (jnp.minimum(i + 1, n_blocks - 1), 0, 0),
                                   memory_space=pltpu.SMEM),
                      pl.BlockSpec(memory_space=pl.ANY),
                      pl.BlockSpec((1, D_MODEL, 2 * D_FF), wmap), pl.BlockSpec((1, 1, 2 * D_FF), wmap),
                      pl.BlockSpec((1, D_FF, D_MODEL), wmap), pl.BlockSpec((1, 1, D_MODEL), wmap)],
            out_specs=pl.BlockSpec((bm, D_MODEL), lambda i, be, nb: (i, 0)),
            scratch_shapes=[pltpu.VMEM((2, bm, D_MODEL), F32), pltpu.SemaphoreType.DMA((2,))],
        ),
        out_shape=jax.ShapeDtypeStruct((n_blocks * bm, D_MODEL), F32),
        compiler_params=_cp(("arbitrary",)),
        name="experts",
    )(block_e, nblk, slot_tok, slot_tok, h, lw["w_up"], lw["b_up"], lw["w_down"], lw["b_down"])


def _combine_kernel(tm, d0_ref, dn_ref, x_ref, mf_ref, g2_ref, yb_hbm, o_ref, ybuf, sem):
    i = pl.program_id(0)
    n = pl.num_programs(0)
    slot = i % 2
    rows = tm * TOP_K

    def fetch(d_ref, sl):
        def body(r, c):
            _row_copy(yb_hbm, d_ref[0, 0, r], ybuf, sl, r, sem).start()
            return c

        lax.fori_loop(0, rows, body, 0, unroll=8)

    @pl.when(i == 0)
    def _():
        fetch(d0_ref, 0)

    @pl.when(i + 1 < n)
    def _():
        fetch(dn_ref, 1 - slot)

    def wait_row(r, c):
        _row_copy(yb_hbm, 0, ybuf, slot, r, sem).wait()
        return c

    lax.fori_loop(0, rows, wait_row, 0, unroll=8)
    mf = mf_ref[...]
    y = jnp.zeros((tm, D_MODEL), F32)
    for kk in range(TOP_K):
        y = y + ybuf[slot, kk * tm:(kk + 1) * tm, :] * mf[:, kk:kk + 1]
    o_ref[...] = x_ref[...] + g2_ref[0] * y


def _combine(dest, x, mf, g2, yb, tm, tiles_per_mod):
    n = x.shape[0]
    n_tiles = n // tm
    r = g2.shape[1]
    row = lambda i: (i, 0)
    d_shape = (1, 1, tm * TOP_K)
    return pl.pallas_call(
        functools.partial(_combine_kernel, tm),
        grid=(n_tiles,),
        in_specs=[pl.BlockSpec(d_shape, lambda i: (0, 0, 0), memory_space=pltpu.SMEM),
                  pl.BlockSpec(d_shape, lambda i: (jnp.minimum(i + 1, n_tiles - 1), 0, 0),
                               memory_space=pltpu.SMEM),
                  pl.BlockSpec((tm, D_MODEL), row), pl.BlockSpec((tm, LANES), row),
                  pl.BlockSpec((1, r, D_MODEL), lambda i: (i // tiles_per_mod, 0, 0)),
                  pl.BlockSpec(memory_space=pl.ANY)],
        out_specs=pl.BlockSpec((tm, D_MODEL), row),
        out_shape=jax.ShapeDtypeStruct((n, D_MODEL), F32),
        scratch_shapes=[pltpu.VMEM((2, TOP_K * tm, D_MODEL), F32), pltpu.SemaphoreType.DMA((2,))],
        compiler_params=_cp(("arbitrary",)),
        name="combine",
    )(dest, dest, x, mf, g2, yb)


def _rope_cs(pos, rot, theta):
    inv = 1.0 / (theta ** (jnp.arange(0, rot, 2, dtype=F32) / rot))
    ang = pos.astype(F32)[:, None] * inv[None, :]
    return jnp.cos(ang), jnp.sin(ang)


def _lane_tables(pos):
    t = pos.shape[0]
    c, s = _rope_cs(pos, DA_ROT, DA_THETA)
    h = DA_ROT // 2
    one = jnp.ones((t, DA_HD - DA_ROT), F32)
    zero = lambda w: jnp.zeros((t, w), F32)
    da_c = jnp.concatenate([c, c, one], axis=1)
    da_sa = jnp.concatenate([-s, zero(DA_HD - h)], axis=1)
    da_sb = jnp.concatenate([zero(h), s, zero(DA_HD - DA_ROT)], axis=1)
    da = [jnp.concatenate([a, a], axis=1) for a in (da_c, da_sa, da_sb)]
    c, s = _rope_cs(pos, MLA_ROPE, MLA_THETA)
    h = MLA_ROPE // 2
    tail = LANES - MLA_NOPE - MLA_ROPE
    ml_c = jnp.concatenate([jnp.ones((t, MLA_NOPE), F32), c, c, jnp.ones((t, tail), F32)], axis=1)
    ml_sa = jnp.concatenate([zero(MLA_NOPE), -s, zero(h + tail)], axis=1)
    ml_sb = jnp.concatenate([zero(MLA_NOPE + h), s, zero(tail)], axis=1)
    return da + [ml_c, ml_sa, ml_sb]


def _selection_constants(dec):
    sel_nope = np.zeros((MLA_HEADS * LANES, MLA_HEADS * MLA_NOPE), np.float32)
    sel_rope = np.zeros((MLA_HEADS * LANES, MLA_HEADS * MLA_ROPE), np.float32)
    fold = np.zeros((MLA_HEADS * MLA_ROPE, MLA_ROPE), np.float32)
    expand = np.zeros((MLA_HEADS * MLA_NOPE, LANES), np.float32)
    for h in range(MLA_HEADS):
        for d in range(MLA_NOPE):
            sel_nope[h * LANES + d, h * MLA_NOPE + d] = 1.0
            expand[h * MLA_NOPE + d, h * dec:(h + 1) * dec] = 1.0
        for d in range(MLA_ROPE):
            sel_rope[h * LANES + MLA_NOPE + d, h * MLA_ROPE + d] = 1.0
            fold[h * MLA_ROPE + d, d] = 1.0
    return {"sel_nope": jnp.asarray(sel_nope, BF16), "sel_rope": jnp.asarray(sel_rope, BF16),
            "fold_rope": jnp.asarray(fold, BF16), "expand_head": jnp.asarray(expand, BF16)}


def _pad_heads(w, real):
    pad = [(0, 0)] * (w.ndim - 1) + [(0, LANES - real)]
    return jnp.pad(w, pad).reshape(w.shape[:-2] + (w.shape[-2] * LANES,))


def _prep_weights(P):
    L = DEPTH
    splits = np.cumsum(IN_WIDTHS)[:-1].tolist()
    wq, wk, wv, wglu, wqa, wkva, wkr, wgate = jnp.split(P["w_in"], splits, axis=-1)
    wkr = jnp.pad(wkr, ((0, 0), (0, 0), (MLA_NOPE, LANES - MLA_NOPE - MLA_ROPE)))
    w_in = jnp.concatenate([wq, wk, wv, wglu, wqa, wkva, wkr, wgate], axis=-1).astype(BF16)
    w_qb = _pad_heads(P["mla_w_qb"].reshape(L, MLA_Q_LORA, MLA_HEADS, MLA_NOPE + MLA_ROPE),
                      MLA_NOPE + MLA_ROPE).astype(BF16)
    w_uk = _pad_heads(P["mla_w_uk"], MLA_NOPE).astype(BF16)
    w_uv = _pad_heads(P["mla_w_uv"], MLA_V).astype(BF16)
    w_uk_c = P["mla_w_uk"].reshape(L, MLA_KV_LORA, MLA_HEADS * MLA_NOPE).astype(BF16)
    w_uv_c = P["mla_w_uv"].reshape(L, MLA_KV_LORA, MLA_HEADS * MLA_V).astype(BF16)
    wb = P["w_branch"].astype(BF16)
    w_b2p = jnp.pad(wb[:, 2].reshape(L, MLA_HEADS, MLA_V, D_MODEL),
                    ((0, 0), (0, 0), (0, LANES - MLA_V), (0, 0))).reshape(L, MLA_HEADS * LANES, D_MODEL)
    z = lambda w: jnp.zeros((L, w), F32)
    g_qm = jnp.concatenate([P["mla_g_qn"], P["mla_g_qr"], z(LANES - MLA_NOPE - MLA_ROPE)], axis=-1)
    g_kr = jnp.concatenate([z(MLA_NOPE), P["mla_g_kr"], z(LANES - MLA_NOPE - MLA_ROPE)], axis=-1)
    g_kn = jnp.concatenate([P["mla_g_kn"], z(LANES - MLA_NOPE)], axis=-1)
    w_router = jnp.pad(P["moe_w_router"], ((0, 0), (0, 0), (0, LANES - N_EXPERTS))).astype(BF16)
    b_router = jnp.concatenate([P["moe_b_router"].astype(F32), jnp.full((L, LANES - N_EXPERTS), NEG, F32)],
                               axis=-1)
    w_up = P["moe_w_up"].astype(BF16)
    w_down = P["moe_w_down"].astype(BF16)
    w_out = P["w_out"].astype(BF16)
    lp = P["da_lambda"].astype(F32)
    lam_dyn = jnp.exp(jnp.sum(lp[:, 0] * lp[:, 1], axis=-1)) - jnp.exp(jnp.sum(lp[:, 2] * lp[:, 3], axis=-1))
    row = lambda a: a[None, :]
    layers = []
    for l in range(L):
        lam_init = 0.8 - 0.6 * math.exp(-0.3 * l)
        layers.append({
            "lam": (lam_dyn[l] + lam_init).reshape(1), "lam_scale": 1.0 - lam_init,
            "g_mix": row(P["g_norm_mix"][l]), "g_ffn": row(P["g_norm_ffn"][l]),
            "w_in": w_in[l], "g_q": P["da_g_q"][l].reshape(1, LANES), "g_k": P["da_g_k"][l].reshape(1, LANES),
            "g_qa": row(P["mla_g_qa"][l]), "g_kva": row(P["mla_g_kva"][l]), "g_kr": row(g_kr[l]),
            "g_qm": row(g_qm[l]), "g_kn": row(g_kn[l]), "g_kn_t": row(jnp.tile(P["mla_g_kn"][l], MLA_HEADS)),
            "w_qb": w_qb[l], "w_uk": w_uk[l], "w_uv": w_uv[l], "w_uk_c": w_uk_c[l], "w_uv_c": w_uv_c[l],
            "cv_w": P["cv_w_dw"][l], "cv_b": row(P["cv_b_dw"][l]), "cv_g": row(P["cv_ln_g"][l]),
            "cv_bb": row(P["cv_ln_b"][l]),
            "g_out": row(P["da_g_out"][l]), "w_b0": wb[l, 0], "w_b1": wb[l, 1], "w_b2_c": wb[l, 2],
            "w_b2_p": w_b2p[l], "w_out": w_out[l], "w_router": w_router[l], "b_router": row(b_router[l]),
            "w_up": w_up[l], "b_up": P["moe_b_up"][l][:, None, :], "w_down": w_down[l],
            "b_down": P["moe_b_down"][l][:, None, :],
        })
    return layers


def _moe_dispatch(mi, cnt, n, bm):
    e = mi[:, :TOP_K]
    rank = mi[:, TOP_K:2 * TOP_K]
    counts = cnt[0, :N_EXPERTS].astype(jnp.int32)
    padded = (counts + bm - 1) // bm * bm
    pend = jnp.cumsum(padded)
    pstart = pend - padded
    dest = pstart[e] + rank
    n_blocks = -(-(n * TOP_K) // bm) + N_EXPERTS
    tok = jnp.broadcast_to(jnp.arange(n, dtype=jnp.int32)[:, None], (n, TOP_K))
    slot_tok = jnp.zeros((n_blocks * bm,), jnp.int32).at[dest.reshape(-1)].set(tok.reshape(-1))
    block_e = jnp.minimum(jnp.searchsorted(pend, jnp.arange(n_blocks, dtype=jnp.int32) * bm, side="right"),
                          N_EXPERTS - 1).astype(jnp.int32)
    nblk = (pend[-1] // bm).astype(jnp.int32).reshape(1)
    return dest, slot_tok.reshape(n_blocks, 1, bm), block_e, nblk, n_blocks


def _trunk(x3, mod, layers, consts, past):
    b, t, _ = x3.shape
    n = b * t
    x = x3.reshape(n, D_MODEL)
    if past is None:
        start = 0
        tm = min(256, t)
        nt, nb, r, tiles_per_mod = t // tm, b, 1, t // tm
    else:
        kpool, vpool, cpool, rpool, sconv, pt = past
        start = pt.shape[1] * PAGE_SIZE
        tm = min(256, n)
        nt, nb, r, tiles_per_mod = 1, n // tm, tm, 1
    pos = start + jnp.arange(t, dtype=jnp.int32)
    tabs = _lane_tables(pos)
    if past is None:
        tabs = [a.reshape(nt, tm, LANES) for a in tabs]
    else:
        tabs = [jnp.tile(a, (tm // t, 1)).reshape(1, tm, LANES) for a in tabs]
    bm = 512 if n * TOP_K >= 32768 else 128

    def rows(a):
        if past is None:
            return a.reshape(b, 1, D_MODEL)
        return jnp.repeat(a, t, axis=0).reshape(nb, tm, D_MODEL)

    states = []
    for l, lw in enumerate(layers):
        sh1, sc1, g1, sh2, sc2, g2 = [rows(a) for a in jnp.split(mod[l], 6, axis=-1)]
        outs = _token_prep(x, sc1, sh1, tabs, lw, nt, nb, tm, past is None)
        qda, k, v, u, qmla, ckv, kr, gate = outs[:8]
        u3 = u.reshape(b, t, CONV_C)
        if past is None:
            kcat, vh, kb, vb = outs[8:]
            oda = _flash(lw["lam"], qda, kb, vb, b, t, DA_HEADS, True, True, F32)
            omla = _flash(lw["lam"], qmla, kcat, vh, b, t, MLA_HEADS, False, False, BF16)
            w_b2 = lw["w_b2_p"]
            prev = jnp.zeros((b, CONV_W - 1, CONV_C), F32)
        else:
            oda = _paged_da(l, pt, lw["lam"], qda, k, v, kpool, vpool, t)
            omla = _paged_mla(l, pt, qmla, ckv, kr, lw, consts, cpool, rpool, t)
            w_b2 = lw["w_b2_c"]
            prev = sconv[l]
        ext = jnp.concatenate([prev, u3], axis=1)
        ocv = _conv(ext, lw, t).reshape(n, CONV_C)
        lw2 = dict(lw, w_b2=w_b2)
        x, h, mi, mf, cnt = _merge_route(x, oda, ocv, omla, gate, g1, sc2, sh2, lw2, lw["lam_scale"],
                                         nt, nb, tm)
        dest, slot_tok, block_e, nblk, n_blocks = _moe_dispatch(mi, cnt, n, bm)
        yb = _experts(block_e, nblk, slot_tok, h, lw, bm, n_blocks)
        dest_t = dest.reshape(n // tm, tm, TOP_K).transpose(0, 2, 1).reshape(n // tm, 1, TOP_K * tm)
        x = _combine(dest_t, x, mf, g2, yb, tm, tiles_per_mod)
        states.append((k.reshape(b, t, 2, DA_HD), v.reshape(b, t, DA_VD), ckv.reshape(b, t, MLA_KV_LORA),
                       kr.reshape(b, t, MLA_ROPE), ext[:, -(CONV_W - 1):]))
    stacked = [jnp.stack(s, axis=0) for s in zip(*states)]
    return x.reshape(b, t, D_MODEL), stacked


def kernel(x_prompt, x_sample, c_prompt, c_sample, cache_da_k, cache_da_v, cache_mla_ckv, cache_mla_krope,
           state_conv, page_table, w_ada, b_ada, g_norm_mix, g_norm_ffn, w_in, da_g_q, da_g_k, da_lambda,
           da_g_out, cv_w_dw, cv_b_dw, cv_ln_g, cv_ln_b, mla_g_qa, mla_w_qb, mla_g_kva, mla_w_uk, mla_w_uv,
           mla_g_qn, mla_g_kn, mla_g_qr, mla_g_kr, w_branch, w_out, moe_w_router, moe_b_router, moe_w_up,
           moe_b_up, moe_w_down, moe_b_down):
    P = dict(w_in=w_in, da_g_q=da_g_q, da_g_k=da_g_k, da_lambda=da_lambda, da_g_out=da_g_out,
             g_norm_mix=g_norm_mix, g_norm_ffn=g_norm_ffn,
             cv_w_dw=cv_w_dw, cv_b_dw=cv_b_dw, cv_ln_g=cv_ln_g, cv_ln_b=cv_ln_b,
             mla_g_qa=mla_g_qa, mla_w_qb=mla_w_qb, mla_g_kva=mla_g_kva, mla_w_uk=mla_w_uk, mla_w_uv=mla_w_uv,
             mla_g_qn=mla_g_qn, mla_g_kn=mla_g_kn, mla_g_qr=mla_g_qr, mla_g_kr=mla_g_kr,
             w_branch=w_branch, w_out=w_out, moe_w_router=moe_w_router, moe_b_router=moe_b_router,
             moe_w_up=moe_w_up, moe_b_up=moe_b_up, moe_w_down=moe_w_down, moe_b_down=moe_b_down)
    layers = _prep_weights(P)
    nbp = c_prompt.shape[0]
    mod = _ada(jnp.concatenate([c_prompt, c_sample], axis=0), w_ada, b_ada)
    dec = x_sample.shape[1]
    consts = _selection_constants(dec)
    kpool = cache_da_k.reshape(cache_da_k.shape[:3] + (2 * DA_HD,))
    past = (kpool, cache_da_v, cache_mla_ckv, cache_mla_krope, state_conv, page_table)
    y_sample, (sk, sv, sc, sr, ss) = _trunk(x_sample, mod[:, nbp:], layers, consts, past)
    y_prompt, (pk, pv, pc, pr, ps) = _trunk(x_prompt, mod[:, :nbp], layers, consts, None)
    return (y_prompt, y_sample, pk, pv, pc, pr, ps, sk, sv, sc, sr, ss)
```

```python
import functools
import math

import numpy as np
import jax
import jax.numpy as jnp
from jax import lax
from jax.experimental import pallas as pl
from jax.experimental.pallas import tpu as pltpu

F32 = jnp.float32
BF16 = jnp.bfloat16

D_MODEL = 1024
DEPTH = 4
PAGE_SIZE = 128
DA_HEADS = 4
DA_HD = 64
DA_VD = 2 * DA_HD
DA_ROT = DA_HD // 4
DA_THETA = 500000.0
CONV_C = 512
CONV_W = 31
MLA_HEADS = 8
MLA_NOPE = 64
MLA_ROPE = 32
MLA_V = 64
MLA_Q_LORA = 384
MLA_KV_LORA = 256
MLA_THETA = 10000.0
N_BRANCH = 3
BRANCH_W = 512
N_EXPERTS = 32
TOP_K = 4
D_FF = D_MODEL
SWIGLU_LIMIT = 7.0
SWIGLU_ALPHA = 1.702
NORM_EPS = 1e-6
IN_WIDTHS = (DA_HEADS * 2 * DA_HD, 2 * DA_HD, DA_VD, 2 * CONV_C, MLA_Q_LORA, MLA_KV_LORA, MLA_ROPE,
             N_BRANCH * D_MODEL)

LANES = 128
VMEM_LIMIT = 56 * 1024 * 1024
NEG = -1e30
LOG2E = math.log2(math.e)

_OFF_Q, _OFF_K, _OFF_V, _OFF_GLU = 0, 512, 640, 768
_OFF_QA, _OFF_KVA, _OFF_KR, _OFF_GATE = 1792, 2176, 2432, 2560
_N_IN_PAD = 2560 + N_BRANCH * D_MODEL


def _cp(sem, vmem=VMEM_LIMIT):
    return pltpu.CompilerParams(dimension_semantics=sem, vmem_limit_bytes=vmem)


def _dot(a, b):
    return jnp.dot(a.astype(BF16), b.astype(BF16), preferred_element_type=F32)


def _dot_nt(a, b):
    return lax.dot_general(a.astype(BF16), b.astype(BF16), (((1,), (1,)), ((), ())),
                           preferred_element_type=F32)


def _dot_tn(a, b):
    return lax.dot_general(a.astype(BF16), b.astype(BF16), (((0,), (0,)), ((), ())),
                           preferred_element_type=F32)


def _lane(shape):
    return lax.broadcasted_iota(jnp.int32, shape, len(shape) - 1)


def _rope_lanes(y, c, sa, sb, half):
    w = y.shape[-1]
    return y * c + pltpu.roll(y, w - half, 1) * sa + pltpu.roll(y, half, 1) * sb


def _ada_kernel(c_ref, w_ref, b_ref, o_ref):
    c = c_ref[...]
    cs = c * jax.nn.sigmoid(c)
    o_ref[0] = _dot(cs, w_ref[0]) + b_ref[0]


def _ada(c, w_ada, b_ada):
    r = c.shape[0]
    tn = 1536
    n = w_ada.shape[-1]
    return pl.pallas_call(
        _ada_kernel,
        grid=(DEPTH, n // tn),
        in_specs=[pl.BlockSpec((r, D_MODEL), lambda l, j: (0, 0)),
                  pl.BlockSpec((1, D_MODEL, tn), lambda l, j: (l, 0, j)),
                  pl.BlockSpec((1, 1, tn), lambda l, j: (l, 0, j))],
        out_specs=pl.BlockSpec((1, r, tn), lambda l, j: (l, 0, j)),
        out_shape=jax.ShapeDtypeStruct((DEPTH, r, n), F32),
        compiler_params=_cp(("parallel", "parallel")),
        name="ada",
    )(c, w_ada, b_ada.reshape(DEPTH, 1, n))


def _token_prep_kernel(with_kv, x_ref, sc_ref, sh_ref, gmix_ref, dac_ref, dasa_ref, dasb_ref,
                       mlc_ref, mlsa_ref, mlsb_ref, w_ref, gq_ref, gk_ref, gqa_ref, gkva_ref, gkr_ref,
                       gqm_ref, gkn_ref, wqb_ref, wuk_ref, wuvt_ref, wvt_ref,
                       qda_ref, k_ref, v_ref, u_ref, qmla_ref, ckv_ref, kr_ref, gate_ref, *kv_refs):
    x = x_ref[...]
    xn = x * lax.rsqrt(jnp.mean(x * x, axis=-1, keepdims=True) + NORM_EPS) * gmix_ref[...]
    h = (xn * (1.0 + sc_ref[0]) + sh_ref[0]).astype(BF16)
    tm = x.shape[0]
    lane = _lane((tm, LANES))
    lo = lane < DA_HD

    def half_rms(z, g):
        zz = z * z
        s_lo = jnp.sum(jnp.where(lo, zz, 0.0), axis=-1, keepdims=True)
        s_hi = jnp.sum(jnp.where(lo, 0.0, zz), axis=-1, keepdims=True)
        r = jnp.where(lo, lax.rsqrt(s_lo / DA_HD + NORM_EPS), lax.rsqrt(s_hi / DA_HD + NORM_EPS))
        return z * r * g

    zq = jnp.dot(h, w_ref[:, _OFF_Q:_OFF_Q + 512], preferred_element_type=F32)
    gq = gq_ref[...]
    qs = []
    for hh in range(DA_HEADS):
        sl = slice(hh * LANES, (hh + 1) * LANES)
        y = half_rms(zq[:, sl], gq)
        y = _rope_lanes(y, dac_ref[0], dasa_ref[0], dasb_ref[0], DA_ROT // 2)
        qs.append((y * (DA_HD ** -0.5 * LOG2E)).astype(BF16))
    qda_ref[...] = jnp.concatenate(qs, axis=-1)
    zk = jnp.dot(h, w_ref[:, _OFF_K:_OFF_K + 128], preferred_element_type=F32)
    k_ref[...] = _rope_lanes(half_rms(zk, gk_ref[...]), dac_ref[0], dasa_ref[0], dasb_ref[0], DA_ROT // 2)
    v_ref[...] = jnp.dot(h, w_ref[:, _OFF_V:_OFF_V + 128], preferred_element_type=F32)

    zg = jnp.dot(h, w_ref[:, _OFF_GLU:_OFF_GLU + 1024], preferred_element_type=F32)
    u_ref[...] = zg[:, :CONV_C] * jax.nn.sigmoid(zg[:, CONV_C:])

    zqa = jnp.dot(h, w_ref[:, _OFF_QA:_OFF_QA + MLA_Q_LORA], preferred_element_type=F32)
    cq = zqa * lax.rsqrt(jnp.mean(zqa * zqa, axis=-1, keepdims=True) + NORM_EPS) * gqa_ref[...]
    qh = _dot(cq, wqb_ref[...])
    nope = lane < MLA_NOPE
    ropem = jnp.logical_and(lane >= MLA_NOPE, lane < MLA_NOPE + MLA_ROPE)
    gqm = gqm_ref[...]
    mlc, mlsa, mlsb = mlc_ref[0], mlsa_ref[0], mlsb_ref[0]
    qm = []
    for hh in range(MLA_HEADS):
        z = qh[:, hh * LANES:(hh + 1) * LANES]
        zz = z * z
        s_n = jnp.sum(jnp.where(nope, zz, 0.0), axis=-1, keepdims=True)
        s_r = jnp.sum(jnp.where(ropem, zz, 0.0), axis=-1, keepdims=True)
        r = jnp.where(nope, lax.rsqrt(s_n / MLA_NOPE + NORM_EPS), lax.rsqrt(s_r / MLA_ROPE + NORM_EPS))
        y = _rope_lanes(z * r * gqm, mlc, mlsa, mlsb, MLA_ROPE // 2)
        qm.append((y * ((MLA_NOPE + MLA_ROPE) ** -0.5 * LOG2E)).astype(BF16))
    qmla_ref[...] = jnp.concatenate(qm, axis=-1)

    zkva = jnp.dot(h, w_ref[:, _OFF_KVA:_OFF_KVA + MLA_KV_LORA], preferred_element_type=F32)
    ckv = zkva * lax.rsqrt(jnp.mean(zkva * zkva, axis=-1, keepdims=True) + NORM_EPS) * gkva_ref[...]
    ckv_ref[...] = ckv
    zkr = jnp.dot(h, w_ref[:, _OFF_KR:_OFF_KR + 128], preferred_element_type=F32)
    s_kr = jnp.sum(zkr * zkr, axis=-1, keepdims=True)
    kr = _rope_lanes(zkr * lax.rsqrt(s_kr / MLA_ROPE + NORM_EPS) * gkr_ref[...], mlc, mlsa, mlsb,
                     MLA_ROPE // 2)
    kr_ref[...] = kr[:, MLA_NOPE:MLA_NOPE + MLA_ROPE]

    gate_ref[...] = jax.nn.sigmoid(
        jnp.dot(h, w_ref[:, _OFF_GATE:_OFF_GATE + N_BRANCH * D_MODEL], preferred_element_type=F32)
    ).astype(BF16)

    if with_kv:
        kcat_ref, vht_ref, kb_ref, vbt_ref = kv_refs
        kb_ref[...] = k_ref[...].astype(BF16)
        vbt_ref[...] = _dot_nt(wvt_ref[...], h).astype(BF16)
        vht_ref[...] = _dot_nt(wuvt_ref[...], ckv).astype(BF16)
        kn = _dot(ckv, wuk_ref[...])
        gkn = gkn_ref[...]
        ks = []
        for hh in range(MLA_HEADS):
            z = kn[:, hh * LANES:(hh + 1) * LANES]
            s = jnp.sum(z * z, axis=-1, keepdims=True)
            ks.append((z * lax.rsqrt(s / MLA_NOPE + NORM_EPS) * gkn + kr).astype(BF16))
        kcat_ref[...] = jnp.concatenate(ks, axis=-1)


def _token_prep(x, sc, sh, tabs, lw, nt, nb, tm, with_kv):
    n = x.shape[0]
    r = sc.shape[1]
    row = lambda t, b: (b * nt + t, 0)
    const2 = lambda t, b: (0, 0)
    tab_spec = pl.BlockSpec((1, tm, LANES), lambda t, b: (t, 0, 0))
    mod_spec = pl.BlockSpec((1, r, D_MODEL), lambda t, b: (b, 0, 0))

    def full(a):
        return pl.BlockSpec(a.shape, const2)

    def out(w, dt):
        return jax.ShapeDtypeStruct((n, w), dt), pl.BlockSpec((tm, w), row)

    outs = [out(512, BF16), out(128, F32), out(128, F32), out(CONV_C, F32), out(1024, BF16),
            out(MLA_KV_LORA, F32), out(MLA_ROPE, F32), out(N_BRANCH * D_MODEL, BF16)]
    def out_t(w, dt):
        return jax.ShapeDtypeStruct((w, n), dt), pl.BlockSpec((w, tm), lambda t, b: (0, b * nt + t))

    if with_kv:
        outs += [out(1024, BF16), out_t(1024, BF16), out(128, BF16), out_t(128, BF16)]
    small = [lw["g_q"], lw["g_k"], lw["g_qa"], lw["g_kva"], lw["g_kr"], lw["g_qm"], lw["g_kn"],
             lw["w_qb"], lw["w_uk"], lw["w_uv_t"], lw["w_v_t"]]
    return pl.pallas_call(
        functools.partial(_token_prep_kernel, with_kv),
        grid=(nt, nb),
        in_specs=[pl.BlockSpec((tm, D_MODEL), row), mod_spec, mod_spec, full(lw["g_mix"])]
                 + [tab_spec] * 6 + [full(lw["w_in"])] + [full(a) for a in small],
        out_specs=[o[1] for o in outs],
        out_shape=[o[0] for o in outs],
        compiler_params=_cp(("parallel", "parallel")),
        name="token_prep",
    )(x, sc, sh, lw["g_mix"], *tabs, lw["w_in"], *small)


def _flash_kernel(diff, hp, tk, rc, lam_ref, q_ref, k_ref, vt_ref, o_ref, *scratch):
    qi = pl.program_id(2)
    tq = q_ref.shape[0]
    if diff:
        chains = []
        for c in range(hp):
            q = q_ref[:, c * LANES:(c + 1) * LANES]
            lo = _lane(q.shape) < DA_HD
            zero = jnp.zeros_like(q)
            chains += [(jnp.where(lo, q, zero), 0), (jnp.where(lo, zero, q), 0)]
    else:
        chains = [(q_ref[:, c * LANES:(c + 1) * LANES], c) for c in range(hp)]
    s_refs, p_refs = scratch[:len(chains)], scratch[len(chains):]
    n_chunks = tk // rc
    n_full = (qi * tq) // tk

    def fold8(x):
        return x.reshape(rc // 8, 8, tq)

    def step(j, carry, masked):
        off = pl.multiple_of(j * tk, tk)
        for ci, (qs, c) in enumerate(chains):
            s = _dot_nt(k_ref[pl.ds(off, tk), c * LANES:(c + 1) * LANES], qs)
            if masked:
                k_pos = off + lax.broadcasted_iota(jnp.int32, (tk, tq), 0)
                q_pos = qi * tq + lax.broadcasted_iota(jnp.int32, (tk, tq), 1)
                s = jnp.where(k_pos <= q_pos, s, NEG)
            s_refs[ci][...] = s
        stats = []
        for ci, (m, l, _) in enumerate(carry):
            s_ref, p_ref = s_refs[ci], p_refs[ci]
            mx8 = jnp.full((8, tq), NEG, F32)
            for cc in range(n_chunks):
                mx8 = jnp.maximum(mx8, jnp.max(fold8(s_ref[cc * rc:(cc + 1) * rc, :]), axis=0))
            m2 = jnp.maximum(m, jnp.max(mx8, axis=0, keepdims=True))
            a = jnp.exp2(m - m2)
            l8 = jnp.zeros((8, tq), F32)
            for cc in range(n_chunks):
                p = jnp.exp2(s_ref[cc * rc:(cc + 1) * rc, :] - m2)
                p_ref[cc * rc:(cc + 1) * rc, :] = p.astype(BF16)
                l8 = l8 + jnp.sum(fold8(p), axis=0)
            stats.append((m2, a, a * l + jnp.sum(l8, axis=0, keepdims=True)))
        new = []
        for ci, ((_, c), (_, _, acc)) in enumerate(zip(chains, carry)):
            m2, a, l2 = stats[ci]
            vt = vt_ref[c * LANES:(c + 1) * LANES, pl.ds(off, tk)]
            new.append((m2, l2, a * acc + jnp.dot(vt, p_refs[ci][...], preferred_element_type=F32)))
        return tuple(new)

    init = tuple((jnp.full((1, tq), NEG, F32), jnp.zeros((1, tq), F32), jnp.zeros((LANES, tq), F32))
                 for _ in chains)
    res = lax.fori_loop(0, n_full, lambda j, c: step(j, c, False), init)
    res = step(n_full, res, True)
    outs = [acc / l for (_, l, acc) in res]
    if diff:
        outs = [outs[2 * c] - lam_ref[0] * outs[2 * c + 1] for c in range(hp)]
    o_ref[...] = jnp.concatenate([o.T for o in outs], axis=-1).astype(o_ref.dtype)


def _flash(lam, q, k, vt, nb, t, heads, shared_kv, diff, out_dtype):
    tq = min(256, t)
    tk = min(512, t)
    assert tk % tq == 0 and t % tk == 0
    rc = 64
    nq = t // tq
    hp = 2 if diff else 4
    nc = 2 * hp if diff else hp
    w = hp * LANES
    kvw = LANES if shared_kv else w
    k_map = (lambda b, h, i, lam: (b, 0)) if shared_kv else (lambda b, h, i, lam: (b, h))
    v_map = (lambda b, h, i, lam: (0, b)) if shared_kv else (lambda b, h, i, lam: (h, b))
    return pl.pallas_call(
        functools.partial(_flash_kernel, diff, hp, tk, rc),
        grid_spec=pltpu.PrefetchScalarGridSpec(
            num_scalar_prefetch=1,
            grid=(nb, heads // hp, nq),
            in_specs=[pl.BlockSpec((tq, w), lambda b, h, i, lam: (b * nq + i, h)),
                      pl.BlockSpec((t, kvw), k_map),
                      pl.BlockSpec((kvw, t), v_map)],
            out_specs=pl.BlockSpec((tq, w), lambda b, h, i, lam: (b * nq + i, h)),
            scratch_shapes=[pltpu.VMEM((tk, tq), F32)] * nc + [pltpu.VMEM((tk, tq), BF16)] * nc,
        ),
        out_shape=jax.ShapeDtypeStruct((nb * t, heads * LANES), out_dtype),
        compiler_params=_cp(("parallel", "parallel", "arbitrary")),
        name="flash_da" if diff else "flash_mla",
    )(lam, q, k, vt)


def _page_copies(pt_ref, seq, slot, n_pages, pools, bufs, sems, layer):
    def per_page(p, fn):
        page = pt_ref[seq, p]
        window = pl.ds(pl.multiple_of(p * PAGE_SIZE, PAGE_SIZE), PAGE_SIZE)
        for pool, buf, sem in zip(pools, bufs, sems):
            assert buf.shape[1] != buf.shape[2]
            token_major = buf.shape[1] == n_pages * PAGE_SIZE
            dst = buf.at[slot, window] if token_major else buf.at[slot, :, window]
            fn(pltpu.make_async_copy(pool.at[layer, page], dst, sem.at[slot]))

    return per_page


def _fetch_pages(pt_ref, seq, slot, n_pages, pools, bufs, sems, layer):
    per_page = _page_copies(pt_ref, seq, slot, n_pages, pools, bufs, sems, layer)

    def body(p, c):
        per_page(p, lambda cp: cp.start())
        return c

    lax.fori_loop(0, n_pages, body, 0)


def _wait_pages(pt_ref, seq, slot, n_pages, pools, bufs, sems, layer):
    per_page = _page_copies(pt_ref, seq, slot, n_pages, pools, bufs, sems, layer)

    def body(p, c):
        per_page(p, lambda cp: cp.wait())
        return c

    lax.fori_loop(0, n_pages, body, 0)


def _paged_prologue(pt_ref, n_pages, pools, bufs, sems, layer):
    b = pl.program_id(0)
    nb = pl.num_programs(0)
    slot = b % 2

    @pl.when(b == 0)
    def _():
        _fetch_pages(pt_ref, 0, 0, n_pages, pools, bufs, sems, layer)

    @pl.when(b + 1 < nb)
    def _():
        _fetch_pages(pt_ref, b + 1, 1 - slot, n_pages, pools, bufs, sems, layer)

    _wait_pages(pt_ref, b, slot, n_pages, pools, bufs, sems, layer)
    return slot


def _col_softmax_stats(s_ref, n_chunks, chunk, s_new):
    def mx(c, m):
        return jnp.maximum(m, jnp.max(s_ref[pl.ds(pl.multiple_of(c * chunk, chunk), chunk), :], axis=0,
                                      keepdims=True))

    return lax.fori_loop(0, n_chunks, mx, jnp.max(s_new, axis=0, keepdims=True))


def _paged_da_kernel(layer, n_pages, chunk, dec, pt_ref, lam_ref, q_ref, kn_ref, vn_ref, ktpool, vpool,
                     o_ref, ktbuf, vbuf, s_ref, ksem, vsem):
    slot = _paged_prologue(pt_ref, n_pages, (ktpool, vpool), (ktbuf, vbuf), (ksem, vsem), layer)
    tk = n_pages * PAGE_SIZE
    n_chunks = tk // chunk
    lam = lam_ref[0]

    q = q_ref[...].astype(F32)
    lane = _lane((dec, LANES))
    zero = jnp.zeros((dec, LANES), F32)
    pad = jnp.zeros((64 - DA_HEADS * dec, LANES), F32)
    rows = []
    for mp in range(2):
        keep = (lane < DA_HD) if mp == 0 else (lane >= DA_HD)
        rows += [jnp.where(keep, q[:, hh * LANES:(hh + 1) * LANES], zero) for hh in range(DA_HEADS)]
        rows.append(pad)
    qx = jnp.concatenate(rows, axis=0).astype(BF16)

    def lane_tiles(x, op):
        parts = [x[:, i * LANES:(i + 1) * LANES] for i in range(x.shape[1] // LANES)]
        while len(parts) > 1:
            parts = [op(parts[i], parts[i + 1]) for i in range(0, len(parts), 2)]
        return parts[0]

    def score(c, mx):
        off = pl.multiple_of(c * chunk, chunk)
        s = jnp.dot(qx, ktbuf[slot, :, pl.ds(off, chunk)].astype(BF16), preferred_element_type=F32)
        s_ref[:, pl.ds(off, chunk)] = s
        return jnp.maximum(mx, lane_tiles(s, jnp.maximum))

    mx = lax.fori_loop(0, n_chunks, score, jnp.full((LANES, LANES), NEG, F32))
    zrows = jnp.zeros((LANES - dec, LANES), F32)
    s_new = _dot_nt(qx, jnp.concatenate([kn_ref[...], zrows], axis=0))
    jcol = _lane((LANES, LANES))
    irow = lax.broadcasted_iota(jnp.int32, (LANES, LANES), 0) % dec
    s_new = jnp.where(jnp.logical_and(jcol < dec, jcol <= irow), s_new, NEG)
    m = jnp.max(jnp.maximum(mx, s_new), axis=-1, keepdims=True)
    p_new = jnp.exp2(s_new - m)

    def pv(c, carry):
        ls, acc = carry
        off = pl.multiple_of(c * chunk, chunk)
        p = jnp.exp2(s_ref[:, pl.ds(off, chunk)] - m)
        return ls + lane_tiles(p, jnp.add), acc + _dot(p, vbuf[slot, pl.ds(off, chunk), :])

    ls, acc = lax.fori_loop(0, n_chunks, pv,
                            (p_new, _dot(p_new, jnp.concatenate([vn_ref[...], zrows], axis=0))))
    o = acc / jnp.sum(ls, axis=-1, keepdims=True)
    half = LANES // 2
    o_ref[...] = jnp.concatenate(
        [o[hh * dec:(hh + 1) * dec, :] - lam * o[half + hh * dec:half + (hh + 1) * dec, :]
         for hh in range(DA_HEADS)], axis=-1)


def _paged_da(layer, pt, lam, q, k_new, v_new, ktpool, vpool, dec):
    ns, n_pages = pt.shape
    tk = n_pages * PAGE_SIZE
    chunk = min(1024, tk)
    seq = lambda b, pt, lam: (b, 0)
    return pl.pallas_call(
        functools.partial(_paged_da_kernel, layer, n_pages, chunk, dec),
        grid_spec=pltpu.PrefetchScalarGridSpec(
            num_scalar_prefetch=2,
            grid=(ns,),
            in_specs=[pl.BlockSpec((dec, 512), seq), pl.BlockSpec((dec, LANES), seq),
                      pl.BlockSpec((dec, LANES), seq),
                      pl.BlockSpec(memory_space=pl.ANY), pl.BlockSpec(memory_space=pl.ANY)],
            out_specs=pl.BlockSpec((dec, 512), seq),
            scratch_shapes=[pltpu.VMEM((2, LANES, tk), F32), pltpu.VMEM((2, tk, LANES), F32),
                            pltpu.VMEM((LANES, tk), F32),
                            pltpu.SemaphoreType.DMA((2,)), pltpu.SemaphoreType.DMA((2,))],
        ),
        out_shape=jax.ShapeDtypeStruct((ns * dec, 512), F32),
        compiler_params=_cp(("arbitrary",)),
        name="paged_da",
    )(pt, lam, q, k_new, v_new, ktpool, vpool)


def _paged_mla_kernel(layer, n_pages, chunk, dec, pt_ref, q_ref, cn_ref, rn_ref, wuk_ref, wuv_ref, gkn_ref,
                      seln_ref, selr_ref, fold_ref, foldt_ref, expand_ref, cpool, rpool,
                      o_ref, cbuf, rbuf, s_ref, csem, rsem):
    slot = _paged_prologue(pt_ref, n_pages, (cpool, rpool), (cbuf, rbuf), (csem, rsem), layer)
    tk = n_pages * PAGE_SIZE
    n_chunks = tk // chunk
    nrow = MLA_HEADS * dec

    q = q_ref[...]
    qn = jnp.dot(q, seln_ref[...], preferred_element_type=F32) * gkn_ref[...]
    qr = jnp.dot(q, selr_ref[...], preferred_element_type=F32)
    reps = LANES // dec
    rowh = lax.broadcasted_iota(jnp.int32, (LANES, 1), 0) // dec
    qn_t = jnp.concatenate([qn] * reps, axis=0)
    qx = jnp.where(_lane(qn_t.shape) // MLA_NOPE == rowh, qn_t, 0.0).astype(BF16)
    qr_t = jnp.concatenate([qr] * reps, axis=0)
    qr_m = jnp.where(_lane(qr_t.shape) // MLA_ROPE == rowh, qr_t, 0.0).astype(BF16)
    qrx = jnp.dot(qr_m, fold_ref[...], preferred_element_type=F32).astype(BF16)
    qrx_t = _dot_nt(foldt_ref[...], qr_m).astype(BF16)
    wuk = wuk_ref[...]
    expand = expand_ref[...]
    w_all = jnp.concatenate([wuk, _dot_nt(wuk, qx).astype(BF16)], axis=1)
    n_kn = MLA_HEADS * MLA_NOPE

    def scores(c_lat, rope_term):
        both = _dot(c_lat, w_all)
        kn = both[:, :n_kn]
        ssq = _dot(kn * kn, expand)
        return both[:, n_kn:] * lax.rsqrt(ssq / MLA_NOPE + NORM_EPS) + rope_term

    def score(c, carry):
        off = pl.multiple_of(c * chunk, chunk)
        rope_term = _dot_tn(rbuf[slot, :, pl.ds(off, chunk)], qrx_t)
        s_ref[pl.ds(off, chunk), :] = scores(cbuf[slot, pl.ds(off, chunk), :], rope_term)
        return carry

    lax.fori_loop(0, n_chunks, score, 0)
    c_new = cn_ref[...]
    s_new = scores(c_new, _dot_nt(rn_ref[...], qrx))
    jrow = lax.broadcasted_iota(jnp.int32, (dec, LANES), 0)
    icol = _lane((dec, LANES)) % dec
    s_new = jnp.where(jrow <= icol, s_new, NEG)
    m = _col_softmax_stats(s_ref, n_chunks, chunk, s_new)
    p_new = jnp.exp2(s_new - m)

    def pv(c, carry):
        l, acc = carry
        off = pl.multiple_of(c * chunk, chunk)
        p = jnp.exp2(s_ref[pl.ds(off, chunk), :] - m)
        return (l + jnp.sum(p, axis=0, keepdims=True),
                acc + _dot_tn(p, cbuf[slot, pl.ds(off, chunk), :]))

    l, acc = lax.fori_loop(0, n_chunks, pv,
                           (jnp.sum(p_new, axis=0, keepdims=True), _dot_tn(p_new, c_new)))
    eye = (lax.broadcasted_iota(jnp.int32, (LANES, LANES), 0) == _lane((LANES, LANES))).astype(F32)
    l_col = jnp.sum(eye * l, axis=-1, keepdims=True)
    o_lat = acc / l_col
    wuv = wuv_ref[...]
    outs = []
    for hh in range(MLA_HEADS):
        outs.append(_dot(o_lat[hh * dec:(hh + 1) * dec, :], wuv[:, hh * MLA_V:(hh + 1) * MLA_V]))
    o_ref[...] = jnp.concatenate(outs, axis=-1).astype(o_ref.dtype)
    del nrow


def _paged_mla(layer, pt, q, c_new, r_new, lw, consts, cpool, rpool, dec):
    ns, n_pages = pt.shape
    tk = n_pages * PAGE_SIZE
    chunk = min(1024, tk)
    seq = lambda b, pt: (b, 0)
    const2 = lambda b, pt: (0, 0)
    small = [lw["w_uk_c"], lw["w_uv_c"], lw["g_kn_t"], consts["sel_nope"], consts["sel_rope"],
             consts["fold_rope"], consts["fold_rope_t"], consts["expand_head"]]
    return pl.pallas_call(
        functools.partial(_paged_mla_kernel, layer, n_pages, chunk, dec),
        grid_spec=pltpu.PrefetchScalarGridSpec(
            num_scalar_prefetch=1,
            grid=(ns,),
            in_specs=[pl.BlockSpec((dec, 1024), seq), pl.BlockSpec((dec, MLA_KV_LORA), seq),
                      pl.BlockSpec((dec, MLA_ROPE), seq)]
                     + [pl.BlockSpec(a.shape, const2) for a in small]
                     + [pl.BlockSpec(memory_space=pl.ANY), pl.BlockSpec(memory_space=pl.ANY)],
            out_specs=pl.BlockSpec((dec, MLA_HEADS * MLA_V), seq),
            scratch_shapes=[pltpu.VMEM((2, tk, MLA_KV_LORA), F32), pltpu.VMEM((2, MLA_ROPE, tk), F32),
                            pltpu.VMEM((tk, LANES), F32),
                            pltpu.SemaphoreType.DMA((2,)), pltpu.SemaphoreType.DMA((2,))],
        ),
        out_shape=jax.ShapeDtypeStruct((ns * dec, MLA_HEADS * MLA_V), BF16),
        compiler_params=_cp(("arbitrary",)),
        name="paged_mla",
    )(pt, q, c_new, r_new, *small, cpool, rpool)


def _conv_kernel(t, tt, ext_ref, w_ref, b_ref, g_ref, bb_ref, o_ref):
    w = w_ref[...]

    def tile(i, carry):
        base = pl.multiple_of(i * tt, tt)
        win = ext_ref[0, pl.ds(base, tt + CONV_W - 1), :]
        acc = jnp.zeros((tt, CONV_C), F32)
        for j in range(CONV_W):
            acc = acc + win[j:j + tt, :] * w[j:j + 1, :]
        y = acc + b_ref[...]
        mu = jnp.mean(y, axis=-1, keepdims=True)
        yc = y - mu
        yn = yc * lax.rsqrt(jnp.mean(yc * yc, axis=-1, keepdims=True) + NORM_EPS) * g_ref[...] + bb_ref[...]
        o_ref[0, pl.ds(base, tt), :] = (yn * jax.nn.sigmoid(yn)).astype(o_ref.dtype)
        return carry

    lax.fori_loop(0, t // tt, tile, 0)


def _conv(ext, lw, t):
    nb = ext.shape[0]
    tt = min(32, t)
    vec = pl.BlockSpec((1, CONV_C), lambda b: (0, 0))
    return pl.pallas_call(
        functools.partial(_conv_kernel, t, tt),
        grid=(nb,),
        in_specs=[pl.BlockSpec((1, CONV_W - 1 + t, CONV_C), lambda b: (b, 0, 0)),
                  pl.BlockSpec((CONV_W, CONV_C), lambda b: (0, 0)), vec, vec, vec],
        out_specs=pl.BlockSpec((1, t, CONV_C), lambda b: (b, 0, 0)),
        out_shape=jax.ShapeDtypeStruct((nb, t, CONV_C), BF16),
        compiler_params=_cp(("parallel",)),
        name="conv",
    )(ext, lw["cv_w"], lw["cv_b"], lw["cv_g"], lw["cv_bb"])


def _merge_route_kernel(lam_scale, x_ref, oda_ref, ocv_ref, omla_ref, gate_ref, g1_ref, sc2_ref, sh2_ref,
                        gout_ref, gffn_ref, wb0_ref, wb1_ref, wb2_ref, wout_ref, wr_ref, br_ref, tri_ref,
                        xo_ref, h_ref, mi_ref, mf_ref, cnt_ref, carry_ref):
    first = jnp.logical_and(pl.program_id(0) == 0, pl.program_id(1) == 0)

    @pl.when(first)
    def _():
        carry_ref[...] = jnp.zeros_like(carry_ref)

    oda = oda_ref[...]
    gout = gout_ref[...]
    das = []
    for hh in range(DA_HEADS):
        z = oda[:, hh * LANES:(hh + 1) * LANES]
        r = lax.rsqrt(jnp.mean(z * z, axis=-1, keepdims=True) + NORM_EPS)
        das.append((z * r * gout * lam_scale).astype(BF16))
    o_da = jnp.concatenate(das, axis=-1)
    gate = gate_ref[...].astype(F32)
    merged = (gate[:, :D_MODEL] * jnp.dot(o_da, wb0_ref[...], preferred_element_type=F32)
              + gate[:, D_MODEL:2 * D_MODEL] * jnp.dot(ocv_ref[...], wb1_ref[...], preferred_element_type=F32)
              + gate[:, 2 * D_MODEL:] * jnp.dot(omla_ref[...], wb2_ref[...], preferred_element_type=F32))
    x = x_ref[...] + g1_ref[0] * _dot(merged, wout_ref[...])
    xo_ref[...] = x
    xn = x * lax.rsqrt(jnp.mean(x * x, axis=-1, keepdims=True) + NORM_EPS) * gffn_ref[...]
    h = xn * (1.0 + sc2_ref[0]) + sh2_ref[0]
    h_ref[...] = h

    tm = x.shape[0]
    logits = _dot(h, wr_ref[...]) + br_ref[...]
    lane = _lane((tm, LANES))
    work = logits
    sel = jnp.zeros((tm, LANES), F32)
    vals, idxs = [], []
    for _ in range(TOP_K):
        mx = jnp.max(work, axis=-1, keepdims=True)
        idx = jnp.min(jnp.where(work == mx, lane, LANES), axis=-1, keepdims=True)
        hit = lane == idx
        sel = jnp.where(hit, 1.0, sel)
        work = jnp.where(hit, NEG * 2.0, work)
        vals.append(mx)
        idxs.append(idx)
    es = [jnp.exp(v - vals[0]) for v in vals]
    den = es[0] + es[1] + es[2] + es[3]
    rank_all = carry_ref[...] + jnp.dot(tri_ref[...], sel.astype(BF16), preferred_element_type=F32)
    carry_ref[...] = carry_ref[...] + jnp.sum(sel, axis=0, keepdims=True)
    mi = jnp.zeros((tm, LANES), jnp.int32)
    mf = jnp.zeros((tm, LANES), F32)
    for kk in range(TOP_K):
        rk = jnp.sum(jnp.where(lane == idxs[kk], rank_all, 0.0), axis=-1, keepdims=True).astype(jnp.int32)
        mi = jnp.where(lane == kk, idxs[kk], mi)
        mi = jnp.where(lane == TOP_K + kk, rk, mi)
        mf = jnp.where(lane == kk, es[kk] / den, mf)
    mi_ref[...] = mi
    mf_ref[...] = mf
    cnt_ref[...] = carry_ref[...]


def _merge_route(x, oda, ocv, omla, gate, g1, sc2, sh2, lw, lam_scale, nt, nb, tm):
    n = x.shape[0]
    r = g1.shape[1]
    row = lambda t, b: (b * nt + t, 0)
    const2 = lambda t, b: (0, 0)
    mod_spec = pl.BlockSpec((1, r, D_MODEL), lambda t, b: (b, 0, 0))
    tri = (np.arange(tm)[:, None] > np.arange(tm)[None, :]).astype(np.float32)
    tri = jnp.asarray(tri, BF16)
    small = [lw["g_out"], lw["g_ffn"], lw["w_b0"], lw["w_b1"], lw["w_b2"], lw["w_out"], lw["w_router"],
             lw["b_router"], tri]

    def rows(w):
        return pl.BlockSpec((tm, w), row)

    return pl.pallas_call(
        functools.partial(_merge_route_kernel, lam_scale),
        grid=(nt, nb),
        in_specs=[rows(D_MODEL), rows(512), rows(CONV_C), rows(omla.shape[1]), rows(N_BRANCH * D_MODEL),
                  mod_spec, mod_spec, mod_spec] + [pl.BlockSpec(a.shape, const2) for a in small],
        out_specs=[rows(D_MODEL), rows(D_MODEL), rows(LANES), rows(LANES),
                   pl.BlockSpec((1, LANES), const2)],
        out_shape=[jax.ShapeDtypeStruct((n, D_MODEL), F32), jax.ShapeDtypeStruct((n, D_MODEL), F32),
                   jax.ShapeDtypeStruct((n, LANES), jnp.int32), jax.ShapeDtypeStruct((n, LANES), F32),
                   jax.ShapeDtypeStruct((1, LANES), F32)],
        scratch_shapes=[pltpu.VMEM((1, LANES), F32)],
        compiler_params=_cp(("arbitrary", "arbitrary")),
        name="merge_route",
    )(x, oda, ocv, omla, gate, g1, sc2, sh2, *small)


def _row_copy(src, row, buf, slot, r, sem):
    return pltpu.make_async_copy(src.at[pl.ds(row, 1)], buf.at[slot, pl.ds(r, 1)], sem.at[slot])


def _experts_kernel(bm, be_ref, nblk_ref, tok0_ref, tokn_ref, h_hbm, wup_ref, bup_ref, wdn_ref, bdn_ref,
                    y_ref, xbuf, sem):
    i = pl.program_id(0)
    nblk = nblk_ref[0]
    slot = i % 2

    def fetch(tok_ref, sl):
        def body(r, c):
            _row_copy(h_hbm, tok_ref[0, 0, r], xbuf, sl, r, sem).start()
            return c

        lax.fori_loop(0, bm, body, 0, unroll=8)

    @pl.when(jnp.logical_and(i == 0, nblk > 0))
    def _():
        fetch(tok0_ref, 0)

    @pl.when(i + 1 < nblk)
    def _():
        fetch(tokn_ref, 1 - slot)

    @pl.when(i < nblk)
    def _():
        def wait_row(r, c):
            _row_copy(h_hbm, 0, xbuf, slot, r, sem).wait()
            return c

        lax.fori_loop(0, bm, wait_row, 0, unroll=8)
        xg = xbuf[slot].astype(BF16)
        u = jnp.dot(xg, wup_ref[0], preferred_element_type=F32) + bup_ref[0]
        g = jnp.minimum(u[:, :D_FF], SWIGLU_LIMIT)
        lin = jnp.clip(u[:, D_FF:], -SWIGLU_LIMIT, SWIGLU_LIMIT)
        act = g * jax.nn.sigmoid(SWIGLU_ALPHA * g) * (lin + 1.0)
        y_ref[...] = _dot(act, wdn_ref[0]) + bdn_ref[0]

    @pl.when(i >= nblk)
    def _():
        y_ref[...] = jnp.zeros_like(y_ref)


def _experts(block_e, nblk, slot_tok, h, lw, bm, n_blocks):
    def wmap(i, be, nb):
        return (be[i], 0, 0)

    tok_shape = (1, 1, bm)
    return pl.pallas_call(
        functools.partial(_experts_kernel, bm),
        grid_spec=pltpu.PrefetchScalarGridSpec(
            num_scalar_prefetch=2,
            grid=(n_blocks,),
            in_specs=[pl.BlockSpec(tok_shape, lambda i, be, nb: (0, 0, 0), memory_space=pltpu.SMEM),
                      pl.BlockSpec(tok_shape, lambda i, be, nb: (jnp.minimum(i + 1, n_blocks - 1), 0, 0),
                                   memory_space=pltpu.SMEM),
                      pl.BlockSpec(memory_space=pl.ANY),
                      pl.BlockSpec((1, D_MODEL, 2 * D_FF), wmap), pl.BlockSpec((1, 1, 2 * D_FF), wmap),
                      pl.BlockSpec((1, D_FF, D_MODEL), wmap), pl.BlockSpec((1, 1, D_MODEL), wmap)],
            out_specs=pl.BlockSpec((bm, D_MODEL), lambda i, be, nb: (i, 0)),
            scratch_shapes=[pltpu.VMEM((2, bm, D_MODEL), F32), pltpu.SemaphoreType.DMA((2,))],
        ),
        out_shape=jax.ShapeDtypeStruct((n_blocks * bm, D_MODEL), F32),
        compiler_params=_cp(("arbitrary",)),
        name="experts",
    )(block_e, nblk, slot_tok, slot_tok, h, lw["w_up"], lw["b_up"], lw["w_down"], lw["b_down"])


def _combine_kernel(tm, d0_ref, dn_ref, x_ref, mf_ref, g2_ref, yb_hbm, o_ref, ybuf, sem):
    i = pl.program_id(0)
    n = pl.num_programs(0)
    slot = i % 2
    rows = tm * TOP_K

    def fetch(d_ref, sl):
        def body(r, c):
            _row_copy(yb_hbm, d_ref[0, 0, r], ybuf, sl, r, sem).start()
            return c

        lax.fori_loop(0, rows, body, 0, unroll=8)

    @pl.when(i == 0)
    def _():
        fetch(d0_ref, 0)

    @pl.when(i + 1 < n)
    def _():
        fetch(dn_ref, 1 - slot)

    def wait_row(r, c):
        _row_copy(yb_hbm, 0, ybuf, slot, r, sem).wait()
        return c

    lax.fori_loop(0, rows, wait_row, 0, unroll=8)
    mf = mf_ref[...]
    y = jnp.zeros((tm, D_MODEL), F32)
    for kk in range(TOP_K):
        y = y + ybuf[slot, kk * tm:(kk + 1) * tm, :] * mf[:, kk:kk + 1]
    o_ref[...] = x_ref[...] + g2_ref[0] * y


def _combine(dest, x, mf, g2, yb, tm, tiles_per_mod):
    n = x.shape[0]
    n_tiles = n // tm
    r = g2.shape[1]
    row = lambda i: (i, 0)
    d_shape = (1, 1, tm * TOP_K)
    return pl.pallas_call(
        functools.partial(_combine_kernel, tm),
        grid=(n_tiles,),
        in_specs=[pl.BlockSpec(d_shape, lambda i: (0, 0, 0), memory_space=pltpu.SMEM),
                  pl.BlockSpec(d_shape, lambda i: (jnp.minimum(i + 1, n_tiles - 1), 0, 0),
                               memory_space=pltpu.SMEM),
                  pl.BlockSpec((tm, D_MODEL), row), pl.BlockSpec((tm, LANES), row),
                  pl.BlockSpec((1, r, D_MODEL), lambda i: (i // tiles_per_mod, 0, 0)),
                  pl.BlockSpec(memory_space=pl.ANY)],
        out_specs=pl.BlockSpec((tm, D_MODEL), row),
        out_shape=jax.ShapeDtypeStruct((n, D_MODEL), F32),
        scratch_shapes=[pltpu.VMEM((2, TOP_K * tm, D_MODEL), F32), pltpu.SemaphoreType.DMA((2,))],
        compiler_params=_cp(("arbitrary",)),
        name="combine",
    )(dest, dest, x, mf, g2, yb)


def _rope_cs(pos, rot, theta):
    inv = 1.0 / (theta ** (jnp.arange(0, rot, 2, dtype=F32) / rot))
    ang = pos.astype(F32)[:, None] * inv[None, :]
    return jnp.cos(ang), jnp.sin(ang)


def _lane_tables(pos):
    t = pos.shape[0]
    c, s = _rope_cs(pos, DA_ROT, DA_THETA)
    h = DA_ROT // 2
    one = jnp.ones((t, DA_HD - DA_ROT), F32)
    zero = lambda w: jnp.zeros((t, w), F32)
    da_c = jnp.concatenate([c, c, one], axis=1)
    da_sa = jnp.concatenate([-s, zero(DA_HD - h)], axis=1)
    da_sb = jnp.concatenate([zero(h), s, zero(DA_HD - DA_ROT)], axis=1)
    da = [jnp.concatenate([a, a], axis=1) for a in (da_c, da_sa, da_sb)]
    c, s = _rope_cs(pos, MLA_ROPE, MLA_THETA)
    h = MLA_ROPE // 2
    tail = LANES - MLA_NOPE - MLA_ROPE
    ml_c = jnp.concatenate([jnp.ones((t, MLA_NOPE), F32), c, c, jnp.ones((t, tail), F32)], axis=1)
    ml_sa = jnp.concatenate([zero(MLA_NOPE), -s, zero(h + tail)], axis=1)
    ml_sb = jnp.concatenate([zero(MLA_NOPE + h), s, zero(tail)], axis=1)
    return da + [ml_c, ml_sa, ml_sb]


def _selection_constants(dec):
    sel_nope = np.zeros((MLA_HEADS * LANES, MLA_HEADS * MLA_NOPE), np.float32)
    sel_rope = np.zeros((MLA_HEADS * LANES, MLA_HEADS * MLA_ROPE), np.float32)
    fold = np.zeros((MLA_HEADS * MLA_ROPE, MLA_ROPE), np.float32)
    expand = np.zeros((MLA_HEADS * MLA_NOPE, LANES), np.float32)
    for h in range(MLA_HEADS):
        for d in range(MLA_NOPE):
            sel_nope[h * LANES + d, h * MLA_NOPE + d] = 1.0
            expand[h * MLA_NOPE + d, h * dec:(h + 1) * dec] = 1.0
        for d in range(MLA_ROPE):
            sel_rope[h * LANES + MLA_NOPE + d, h * MLA_ROPE + d] = 1.0
            fold[h * MLA_ROPE + d, d] = 1.0
    return {"sel_nope": jnp.asarray(sel_nope, BF16), "sel_rope": jnp.asarray(sel_rope, BF16),
            "fold_rope": jnp.asarray(fold, BF16), "fold_rope_t": jnp.asarray(fold.T.copy(), BF16),
            "expand_head": jnp.asarray(expand, BF16)}


def _pad_heads(w, real):
    pad = [(0, 0)] * (w.ndim - 1) + [(0, LANES - real)]
    return jnp.pad(w, pad).reshape(w.shape[:-2] + (w.shape[-2] * LANES,))


def _prep_weights(P):
    L = DEPTH
    splits = np.cumsum(IN_WIDTHS)[:-1].tolist()
    wq, wk, wv, wglu, wqa, wkva, wkr, wgate = jnp.split(P["w_in"], splits, axis=-1)
    wkr = jnp.pad(wkr, ((0, 0), (0, 0), (MLA_NOPE, LANES - MLA_NOPE - MLA_ROPE)))
    w_in = jnp.concatenate([wq, wk, wv, wglu, wqa, wkva, wkr, wgate], axis=-1).astype(BF16)
    w_qb = _pad_heads(P["mla_w_qb"].reshape(L, MLA_Q_LORA, MLA_HEADS, MLA_NOPE + MLA_ROPE),
                      MLA_NOPE + MLA_ROPE).astype(BF16)
    w_uk = _pad_heads(P["mla_w_uk"], MLA_NOPE).astype(BF16)
    w_uv = _pad_heads(P["mla_w_uv"], MLA_V).astype(BF16)
    w_uk_c = P["mla_w_uk"].reshape(L, MLA_KV_LORA, MLA_HEADS * MLA_NOPE).astype(BF16)
    w_uv_c = P["mla_w_uv"].reshape(L, MLA_KV_LORA, MLA_HEADS * MLA_V).astype(BF16)
    wb = P["w_branch"].astype(BF16)
    w_b2p = jnp.pad(wb[:, 2].reshape(L, MLA_HEADS, MLA_V, D_MODEL),
                    ((0, 0), (0, 0), (0, LANES - MLA_V), (0, 0))).reshape(L, MLA_HEADS * LANES, D_MODEL)
    z = lambda w: jnp.zeros((L, w), F32)
    g_qm = jnp.concatenate([P["mla_g_qn"], P["mla_g_qr"], z(LANES - MLA_NOPE - MLA_ROPE)], axis=-1)
    g_kr = jnp.concatenate([z(MLA_NOPE), P["mla_g_kr"], z(LANES - MLA_NOPE - MLA_ROPE)], axis=-1)
    g_kn = jnp.concatenate([P["mla_g_kn"], z(LANES - MLA_NOPE)], axis=-1)
    w_router = jnp.pad(P["moe_w_router"], ((0, 0), (0, 0), (0, LANES - N_EXPERTS))).astype(BF16)
    b_router = jnp.concatenate([P["moe_b_router"].astype(F32), jnp.full((L, LANES - N_EXPERTS), NEG, F32)],
                               axis=-1)
    w_up = P["moe_w_up"].astype(BF16)
    w_down = P["moe_w_down"].astype(BF16)
    w_out = P["w_out"].astype(BF16)
    lp = P["da_lambda"].astype(F32)
    lam_dyn = jnp.exp(jnp.sum(lp[:, 0] * lp[:, 1], axis=-1)) - jnp.exp(jnp.sum(lp[:, 2] * lp[:, 3], axis=-1))
    row = lambda a: a[None, :]
    layers = []
    for l in range(L):
        lam_init = 0.8 - 0.6 * math.exp(-0.3 * l)
        layers.append({
            "lam": (lam_dyn[l] + lam_init).reshape(1), "lam_scale": 1.0 - lam_init,
            "g_mix": row(P["g_norm_mix"][l]), "g_ffn": row(P["g_norm_ffn"][l]),
            "w_in": w_in[l], "g_q": P["da_g_q"][l].reshape(1, LANES), "g_k": P["da_g_k"][l].reshape(1, LANES),
            "g_qa": row(P["mla_g_qa"][l]), "g_kva": row(P["mla_g_kva"][l]), "g_kr": row(g_kr[l]),
            "g_qm": row(g_qm[l]), "g_kn": row(g_kn[l]), "g_kn_t": row(jnp.tile(P["mla_g_kn"][l], MLA_HEADS)),
            "w_qb": w_qb[l], "w_uk": w_uk[l], "w_uv_t": w_uv[l].T, "w_v_t": wv[l].astype(BF16).T,
            "w_uk_c": w_uk_c[l], "w_uv_c": w_uv_c[l],
            "cv_w": P["cv_w_dw"][l], "cv_b": row(P["cv_b_dw"][l]), "cv_g": row(P["cv_ln_g"][l]),
            "cv_bb": row(P["cv_ln_b"][l]),
            "g_out": row(P["da_g_out"][l]), "w_b0": wb[l, 0], "w_b1": wb[l, 1], "w_b2_c": wb[l, 2],
            "w_b2_p": w_b2p[l], "w_out": w_out[l], "w_router": w_router[l], "b_router": row(b_router[l]),
            "w_up": w_up[l], "b_up": P["moe_b_up"][l][:, None, :], "w_down": w_down[l],
            "b_down": P["moe_b_down"][l][:, None, :],
        })
    return layers


def _moe_dispatch(mi, cnt, n, bm):
    e = mi[:, :TOP_K]
    rank = mi[:, TOP_K:2 * TOP_K]
    counts = cnt[0, :N_EXPERTS].astype(jnp.int32)
    padded = (counts + bm - 1) // bm * bm
    pend = jnp.cumsum(padded)
    pstart = pend - padded
    dest = pstart[e] + rank
    n_blocks = -(-(n * TOP_K) // bm) + N_EXPERTS
    tok = jnp.broadcast_to(jnp.arange(n, dtype=jnp.int32)[:, None], (n, TOP_K))
    slot_tok = jnp.zeros((n_blocks * bm,), jnp.int32).at[dest.reshape(-1)].set(
        tok.reshape(-1), unique_indices=True, mode="promise_in_bounds")
    block_e = jnp.minimum(jnp.searchsorted(pend, jnp.arange(n_blocks, dtype=jnp.int32) * bm, side="right"),
                          N_EXPERTS - 1).astype(jnp.int32)
    nblk = (pend[-1] // bm).astype(jnp.int32).reshape(1)
    return dest, slot_tok.reshape(n_blocks, 1, bm), block_e, nblk, n_blocks


def _trunk(x3, mod, layers, consts, past):
    b, t, _ = x3.shape
    n = b * t
    x = x3.reshape(n, D_MODEL)
    if past is None:
        start = 0
        tm = min(256, t)
        nt, nb, r, tiles_per_mod = t // tm, b, 1, t // tm
    else:
        kpool, vpool, cpool, rpool, sconv, pt = past
        start = pt.shape[1] * PAGE_SIZE
        tm = min(256, n)
        nt, nb, r, tiles_per_mod = 1, n // tm, tm, 1
    pos = start + jnp.arange(t, dtype=jnp.int32)
    tabs = _lane_tables(pos)
    if past is None:
        tabs = [a.reshape(nt, tm, LANES) for a in tabs]
    else:
        tabs = [jnp.tile(a, (tm // t, 1)).reshape(1, tm, LANES) for a in tabs]
    bm = 512 if n * TOP_K >= 32768 else 128

    def rows(a):
        if past is None:
            return a.reshape(b, 1, D_MODEL)
        return jnp.repeat(a, t, axis=0).reshape(nb, tm, D_MODEL)

    states = []
    for l, lw in enumerate(layers):
        sh1, sc1, g1, sh2, sc2, g2 = [rows(a) for a in jnp.split(mod[l], 6, axis=-1)]
        outs = _token_prep(x, sc1, sh1, tabs, lw, nt, nb, tm, past is None)
        qda, k, v, u, qmla, ckv, kr, gate = outs[:8]
        u3 = u.reshape(b, t, CONV_C)
        if past is None:
            kcat, vht, kb, vbt = outs[8:]
            oda = _flash(lw["lam"], qda, kb, vbt, b, t, DA_HEADS, True, True, F32)
            omla = _flash(lw["lam"], qmla, kcat, vht, b, t, MLA_HEADS, False, False, BF16)
            w_b2 = lw["w_b2_p"]
            prev = jnp.zeros((b, CONV_W - 1, CONV_C), F32)
        else:
            oda = _paged_da(l, pt, lw["lam"], qda, k, v, kpool, vpool, t)
            omla = _paged_mla(l, pt, qmla, ckv, kr, lw, consts, cpool, rpool, t)
            w_b2 = lw["w_b2_c"]
            prev = sconv[l]
        ext = jnp.concatenate([prev, u3], axis=1)
        ocv = _conv(ext, lw, t).reshape(n, CONV_C)
        lw2 = dict(lw, w_b2=w_b2)
        x, h, mi, mf, cnt = _merge_route(x, oda, ocv, omla, gate, g1, sc2, sh2, lw2, lw["lam_scale"],
                                         nt, nb, tm)
        dest, slot_tok, block_e, nblk, n_blocks = _moe_dispatch(mi, cnt, n, bm)
        yb = _experts(block_e, nblk, slot_tok, h, lw, bm, n_blocks)
        dest_t = dest.reshape(n // tm, tm, TOP_K).transpose(0, 2, 1).reshape(n // tm, 1, TOP_K * tm)
        x = _combine(dest_t, x, mf, g2, yb, tm, tiles_per_mod)
        states.append((k.reshape(b, t, 2, DA_HD), v.reshape(b, t, DA_VD), ckv.reshape(b, t, MLA_KV_LORA),
                       kr.reshape(b, t, MLA_ROPE), ext[:, -(CONV_W - 1):]))
    stacked = [jnp.stack(s, axis=0) for s in zip(*states)]
    return x.reshape(b, t, D_MODEL), stacked


def kernel(x_prompt, x_sample, c_prompt, c_sample, cache_da_k, cache_da_v, cache_mla_ckv, cache_mla_krope,
           state_conv, page_table, w_ada, b_ada, g_norm_mix, g_norm_ffn, w_in, da_g_q, da_g_k, da_lambda,
           da_g_out, cv_w_dw, cv_b_dw, cv_ln_g, cv_ln_b, mla_g_qa, mla_w_qb, mla_g_kva, mla_w_uk, mla_w_uv,
           mla_g_qn, mla_g_kn, mla_g_qr, mla_g_kr, w_branch, w_out, moe_w_router, moe_b_router, moe_w_up,
           moe_b_up, moe_w_down, moe_b_down):
    P = dict(w_in=w_in, da_g_q=da_g_q, da_g_k=da_g_k, da_lambda=da_lambda, da_g_out=da_g_out,
             g_norm_mix=g_norm_mix, g_norm_ffn=g_norm_ffn,
             cv_w_dw=cv_w_dw, cv_b_dw=cv_b_dw, cv_ln_g=cv_ln_g, cv_ln_b=cv_ln_b,
             mla_g_qa=mla_g_qa, mla_w_qb=mla_w_qb, mla_g_kva=mla_g_kva, mla_w_uk=mla_w_uk, mla_w_uv=mla_w_uv,
             mla_g_qn=mla_g_qn, mla_g_kn=mla_g_kn, mla_g_qr=mla_g_qr, mla_g_kr=mla_g_kr,
             w_branch=w_branch, w_out=w_out, moe_w_router=moe_w_router, moe_b_router=moe_b_router,
             moe_w_up=moe_w_up, moe_b_up=moe_b_up, moe_w_down=moe_w_down, moe_b_down=moe_b_down)
    layers = _prep_weights(P)
    nbp = c_prompt.shape[0]
    mod = _ada(jnp.concatenate([c_prompt, c_sample], axis=0), w_ada, b_ada)
    dec = x_sample.shape[1]
    consts = _selection_constants(dec)
    ktpool = jnp.transpose(cache_da_k, (0, 1, 3, 4, 2)).reshape(cache_da_k.shape[:2] + (2 * DA_HD, PAGE_SIZE))
    rtpool = jnp.transpose(cache_mla_krope, (0, 1, 3, 2))
    past = (ktpool, cache_da_v, cache_mla_ckv, rtpool, state_conv, page_table)
    y_sample, (sk, sv, sc, sr, ss) = _trunk(x_sample, mod[:, nbp:], layers, consts, past)
    y_prompt, (pk, pv, pc, pr, ps) = _trunk(x_prompt, mod[:, :nbp], layers, consts, None)
    return (y_prompt, y_sample, pk, pv, pc, pr, ps, sk, sv, sc, sr, ss)
```

```python
import functools
import math

import numpy as np
import jax
import jax.numpy as jnp
from jax import lax
from jax.experimental import pallas as pl
from jax.experimental.pallas import tpu as pltpu

F32 = jnp.float32
BF16 = jnp.bfloat16

D_MODEL = 1024
DEPTH = 4
PAGE_SIZE = 128
DA_HEADS = 4
DA_HD = 64
DA_VD = 2 * DA_HD
DA_ROT = DA_HD // 4
DA_THETA = 500000.0
CONV_C = 512
CONV_W = 31
MLA_HEADS = 8
MLA_NOPE = 64
MLA_ROPE = 32
MLA_V = 64
MLA_Q_LORA = 384
MLA_KV_LORA = 256
MLA_THETA = 10000.0
N_BRANCH = 3
BRANCH_W = 512
N_EXPERTS = 32
TOP_K = 4
D_FF = D_MODEL
SWIGLU_LIMIT = 7.0
SWIGLU_ALPHA = 1.702
NORM_EPS = 1e-6
IN_WIDTHS = (DA_HEADS * 2 * DA_HD, 2 * DA_HD, DA_VD, 2 * CONV_C, MLA_Q_LORA, MLA_KV_LORA, MLA_ROPE,
             N_BRANCH * D_MODEL)

LANES = 128
SUB = 8
VMEM_LIMIT = 56 * 1024 * 1024
NEG = -1e30
LOG2E = math.log2(math.e)

_OFF_Q, _OFF_K, _OFF_V, _OFF_GLU = 0, 512, 640, 768
_OFF_QA, _OFF_KVA, _OFF_KR, _OFF_GATE = 1792, 2176, 2432, 2560
_N_IN_PAD = 2560 + N_BRANCH * D_MODEL


def _cp(sem, vmem=VMEM_LIMIT):
    return pltpu.CompilerParams(dimension_semantics=sem, vmem_limit_bytes=vmem)


def _dot(a, b):
    return jnp.dot(a.astype(BF16), b.astype(BF16), preferred_element_type=F32)


def _dot_nt(a, b):
    return lax.dot_general(a.astype(BF16), b.astype(BF16), (((1,), (1,)), ((), ())),
                           preferred_element_type=F32)


def _dot_tn(a, b):
    return lax.dot_general(a.astype(BF16), b.astype(BF16), (((0,), (0,)), ((), ())),
                           preferred_element_type=F32)


def _lane(shape):
    return lax.broadcasted_iota(jnp.int32, shape, len(shape) - 1)


def _rope_lanes(y, c, sa, sb, half):
    w = y.shape[-1]
    return y * c + pltpu.roll(y, w - half, 1) * sa + pltpu.roll(y, half, 1) * sb


def _rows_to_tiles(ref, x):
    n = x.shape[0]
    for s in range(SUB):
        ref[pl.ds(s, n, stride=SUB), :] = x[:, s * LANES:(s + 1) * LANES]


def _tiles_to_rows(ref, slot, first, n):
    return jnp.concatenate([ref[slot, pl.ds(first * SUB + s, n, stride=SUB), :] for s in range(SUB)], axis=-1)


def _ada_kernel(c_ref, w_ref, b_ref, o_ref):
    c = c_ref[...]
    cs = c * jax.nn.sigmoid(c)
    o_ref[0] = _dot(cs, w_ref[0]) + b_ref[0]


def _ada(c, w_ada, b_ada):
    r = c.shape[0]
    tn = 1536
    n = w_ada.shape[-1]
    return pl.pallas_call(
        _ada_kernel,
        grid=(DEPTH, n // tn),
        in_specs=[pl.BlockSpec((r, D_MODEL), lambda l, j: (0, 0)),
                  pl.BlockSpec((1, D_MODEL, tn), lambda l, j: (l, 0, j)),
                  pl.BlockSpec((1, 1, tn), lambda l, j: (l, 0, j))],
        out_specs=pl.BlockSpec((1, r, tn), lambda l, j: (l, 0, j)),
        out_shape=jax.ShapeDtypeStruct((DEPTH, r, n), F32),
        compiler_params=_cp(("parallel", "parallel")),
        name="ada",
    )(c, w_ada, b_ada.reshape(DEPTH, 1, n))


def _token_prep_kernel(with_kv, x_ref, sc_ref, sh_ref, gmix_ref, dac_ref, dasa_ref, dasb_ref,
                       mlc_ref, mlsa_ref, mlsb_ref, w_ref, gq_ref, gk_ref, gqa_ref, gkva_ref, gkr_ref,
                       gqm_ref, gkn_ref, wqb_ref, wuk_ref, wuvt_ref, wvt_ref,
                       qda_ref, k_ref, v_ref, u_ref, qmla_ref, ckv_ref, kr_ref, gate_ref, *kv_refs):
    x = x_ref[...]
    xn = x * lax.rsqrt(jnp.mean(x * x, axis=-1, keepdims=True) + NORM_EPS) * gmix_ref[...]
    h = (xn * (1.0 + sc_ref[0]) + sh_ref[0]).astype(BF16)
    tm = x.shape[0]
    lane = _lane((tm, LANES))
    lo = lane < DA_HD

    def half_rms(z, g):
        zz = z * z
        s_lo = jnp.sum(jnp.where(lo, zz, 0.0), axis=-1, keepdims=True)
        s_hi = jnp.sum(jnp.where(lo, 0.0, zz), axis=-1, keepdims=True)
        r = jnp.where(lo, lax.rsqrt(s_lo / DA_HD + NORM_EPS), lax.rsqrt(s_hi / DA_HD + NORM_EPS))
        return z * r * g

    zq = jnp.dot(h, w_ref[:, _OFF_Q:_OFF_Q + 512], preferred_element_type=F32)
    gq = gq_ref[...]
    qs = []
    for hh in range(DA_HEADS):
        sl = slice(hh * LANES, (hh + 1) * LANES)
        y = half_rms(zq[:, sl], gq)
        y = _rope_lanes(y, dac_ref[0], dasa_ref[0], dasb_ref[0], DA_ROT // 2)
        qs.append((y * (DA_HD ** -0.5 * LOG2E)).astype(BF16))
    qda_ref[...] = jnp.concatenate(qs, axis=-1)
    zk = jnp.dot(h, w_ref[:, _OFF_K:_OFF_K + 128], preferred_element_type=F32)
    k_ref[...] = _rope_lanes(half_rms(zk, gk_ref[...]), dac_ref[0], dasa_ref[0], dasb_ref[0], DA_ROT // 2)
    v_ref[...] = jnp.dot(h, w_ref[:, _OFF_V:_OFF_V + 128], preferred_element_type=F32)

    zg = jnp.dot(h, w_ref[:, _OFF_GLU:_OFF_GLU + 1024], preferred_element_type=F32)
    u_ref[...] = zg[:, :CONV_C] * jax.nn.sigmoid(zg[:, CONV_C:])

    zqa = jnp.dot(h, w_ref[:, _OFF_QA:_OFF_QA + MLA_Q_LORA], preferred_element_type=F32)
    cq = zqa * lax.rsqrt(jnp.mean(zqa * zqa, axis=-1, keepdims=True) + NORM_EPS) * gqa_ref[...]
    qh = _dot(cq, wqb_ref[...])
    nope = lane < MLA_NOPE
    ropem = jnp.logical_and(lane >= MLA_NOPE, lane < MLA_NOPE + MLA_ROPE)
    gqm = gqm_ref[...]
    mlc, mlsa, mlsb = mlc_ref[0], mlsa_ref[0], mlsb_ref[0]
    qm = []
    for hh in range(MLA_HEADS):
        z = qh[:, hh * LANES:(hh + 1) * LANES]
        zz = z * z
        s_n = jnp.sum(jnp.where(nope, zz, 0.0), axis=-1, keepdims=True)
        s_r = jnp.sum(jnp.where(ropem, zz, 0.0), axis=-1, keepdims=True)
        r = jnp.where(nope, lax.rsqrt(s_n / MLA_NOPE + NORM_EPS), lax.rsqrt(s_r / MLA_ROPE + NORM_EPS))
        y = _rope_lanes(z * r * gqm, mlc, mlsa, mlsb, MLA_ROPE // 2)
        qm.append((y * ((MLA_NOPE + MLA_ROPE) ** -0.5 * LOG2E)).astype(BF16))
    qmla_ref[...] = jnp.concatenate(qm, axis=-1)

    zkva = jnp.dot(h, w_ref[:, _OFF_KVA:_OFF_KVA + MLA_KV_LORA], preferred_element_type=F32)
    ckv = zkva * lax.rsqrt(jnp.mean(zkva * zkva, axis=-1, keepdims=True) + NORM_EPS) * gkva_ref[...]
    ckv_ref[...] = ckv
    zkr = jnp.dot(h, w_ref[:, _OFF_KR:_OFF_KR + 128], preferred_element_type=F32)
    s_kr = jnp.sum(zkr * zkr, axis=-1, keepdims=True)
    kr = _rope_lanes(zkr * lax.rsqrt(s_kr / MLA_ROPE + NORM_EPS) * gkr_ref[...], mlc, mlsa, mlsb,
                     MLA_ROPE // 2)
    kr_ref[...] = kr[:, MLA_NOPE:MLA_NOPE + MLA_ROPE]

    gate_ref[...] = jax.nn.sigmoid(
        jnp.dot(h, w_ref[:, _OFF_GATE:_OFF_GATE + N_BRANCH * D_MODEL], preferred_element_type=F32)
    ).astype(BF16)

    if with_kv:
        kcat_ref, vht_ref, kb_ref, vbt_ref = kv_refs
        kb_ref[...] = k_ref[...].astype(BF16)
        vbt_ref[...] = _dot_nt(wvt_ref[...], h).astype(BF16)
        vht_ref[...] = _dot_nt(wuvt_ref[...], ckv).astype(BF16)
        kn = _dot(ckv, wuk_ref[...])
        gkn = gkn_ref[...]
        ks = []
        for hh in range(MLA_HEADS):
            z = kn[:, hh * LANES:(hh + 1) * LANES]
            s = jnp.sum(z * z, axis=-1, keepdims=True)
            ks.append((z * lax.rsqrt(s / MLA_NOPE + NORM_EPS) * gkn + kr).astype(BF16))
        kcat_ref[...] = jnp.concatenate(ks, axis=-1)


def _token_prep(x, sc, sh, tabs, lw, nt, nb, tm, with_kv):
    n = x.shape[0]
    r = sc.shape[1]
    row = lambda t, b: (b * nt + t, 0)
    const2 = lambda t, b: (0, 0)
    tab_spec = pl.BlockSpec((1, tm, LANES), lambda t, b: (t, 0, 0))
    mod_spec = pl.BlockSpec((1, r, D_MODEL), lambda t, b: (b, 0, 0))

    def full(a):
        return pl.BlockSpec(a.shape, const2)

    def out(w, dt):
        return jax.ShapeDtypeStruct((n, w), dt), pl.BlockSpec((tm, w), row)

    outs = [out(512, BF16), out(128, F32), out(128, F32), out(CONV_C, F32), out(1024, BF16),
            out(MLA_KV_LORA, F32), out(MLA_ROPE, F32), out(N_BRANCH * D_MODEL, BF16)]
    def out_t(w, dt):
        return jax.ShapeDtypeStruct((w, n), dt), pl.BlockSpec((w, tm), lambda t, b: (0, b * nt + t))

    if with_kv:
        outs += [out(1024, BF16), out_t(1024, BF16), out(128, BF16), out_t(128, BF16)]
    small = [lw["g_q"], lw["g_k"], lw["g_qa"], lw["g_kva"], lw["g_kr"], lw["g_qm"], lw["g_kn"],
             lw["w_qb"], lw["w_uk"], lw["w_uv_t"], lw["w_v_t"]]
    return pl.pallas_call(
        functools.partial(_token_prep_kernel, with_kv),
        grid=(nt, nb),
        in_specs=[pl.BlockSpec((tm, D_MODEL), row), mod_spec, mod_spec, full(lw["g_mix"])]
                 + [tab_spec] * 6 + [full(lw["w_in"])] + [full(a) for a in small],
        out_specs=[o[1] for o in outs],
        out_shape=[o[0] for o in outs],
        compiler_params=_cp(("parallel", "parallel")),
        name="token_prep",
    )(x, sc, sh, lw["g_mix"], *tabs, lw["w_in"], *small)


def _flash_kernel(diff, hp, tk, rc, lam_ref, q_ref, k_ref, vt_ref, o_ref, *scratch):
    qi = pl.program_id(2)
    tq = q_ref.shape[0]
    if diff:
        chains = []
        for c in range(hp):
            q = q_ref[:, c * LANES:(c + 1) * LANES]
            lo = _lane(q.shape) < DA_HD
            zero = jnp.zeros_like(q)
            chains += [(jnp.where(lo, q, zero), 0), (jnp.where(lo, zero, q), 0)]
    else:
        chains = [(q_ref[:, c * LANES:(c + 1) * LANES], c) for c in range(hp)]
    s_refs, p_refs = scratch[:len(chains)], scratch[len(chains):]
    n_chunks = tk // rc
    n_full = (qi * tq) // tk

    def fold8(x):
        return x.reshape(rc // 8, 8, tq)

    def step(j, carry, masked):
        off = pl.multiple_of(j * tk, tk)
        for ci, (qs, c) in enumerate(chains):
            s = _dot_nt(k_ref[pl.ds(off, tk), c * LANES:(c + 1) * LANES], qs)
            if masked:
                k_pos = off + lax.broadcasted_iota(jnp.int32, (tk, tq), 0)
                q_pos = qi * tq + lax.broadcasted_iota(jnp.int32, (tk, tq), 1)
                s = jnp.where(k_pos <= q_pos, s, NEG)
            s_refs[ci][...] = s
        stats = []
        for ci, (m, l, _) in enumerate(carry):
            s_ref, p_ref = s_refs[ci], p_refs[ci]
            mx8 = jnp.full((8, tq), NEG, F32)
            for cc in range(n_chunks):
                mx8 = jnp.maximum(mx8, jnp.max(fold8(s_ref[cc * rc:(cc + 1) * rc, :]), axis=0))
            m2 = jnp.maximum(m, jnp.max(mx8, axis=0, keepdims=True))
            a = jnp.exp2(m - m2)
            l8 = jnp.zeros((8, tq), F32)
            for cc in range(n_chunks):
                p = jnp.exp2(s_ref[cc * rc:(cc + 1) * rc, :] - m2)
                p_ref[cc * rc:(cc + 1) * rc, :] = p.astype(BF16)
                l8 = l8 + jnp.sum(fold8(p), axis=0)
            stats.append((m2, a, a * l + jnp.sum(l8, axis=0, keepdims=True)))
        new = []
        for ci, ((_, c), (_, _, acc)) in enumerate(zip(chains, carry)):
            m2, a, l2 = stats[ci]
            vt = vt_ref[c * LANES:(c + 1) * LANES, pl.ds(off, tk)]
            new.append((m2, l2, a * acc + jnp.dot(vt, p_refs[ci][...], preferred_element_type=F32)))
        return tuple(new)

    init = tuple((jnp.full((1, tq), NEG, F32), jnp.zeros((1, tq), F32), jnp.zeros((LANES, tq), F32))
                 for _ in chains)
    res = lax.fori_loop(0, n_full, lambda j, c: step(j, c, False), init)
    res = step(n_full, res, True)
    outs = [acc / l for (_, l, acc) in res]
    if diff:
        outs = [outs[2 * c] - lam_ref[0] * outs[2 * c + 1] for c in range(hp)]
    o_ref[...] = jnp.concatenate([o.T for o in outs], axis=-1).astype(o_ref.dtype)


def _flash(lam, q, k, vt, nb, t, heads, shared_kv, diff, out_dtype):
    tq = min(256, t)
    tk = min(512, t)
    assert tk % tq == 0 and t % tk == 0
    rc = 64
    nq = t // tq
    hp = 2 if diff else 4
    nc = 2 * hp if diff else hp
    w = hp * LANES
    kvw = LANES if shared_kv else w
    k_map = (lambda b, h, i, lam: (b, 0)) if shared_kv else (lambda b, h, i, lam: (b, h))
    v_map = (lambda b, h, i, lam: (0, b)) if shared_kv else (lambda b, h, i, lam: (h, b))
    return pl.pallas_call(
        functools.partial(_flash_kernel, diff, hp, tk, rc),
        grid_spec=pltpu.PrefetchScalarGridSpec(
            num_scalar_prefetch=1,
            grid=(nb, heads // hp, nq),
            in_specs=[pl.BlockSpec((tq, w), lambda b, h, i, lam: (b * nq + i, h)),
                      pl.BlockSpec((t, kvw), k_map),
                      pl.BlockSpec((kvw, t), v_map)],
            out_specs=pl.BlockSpec((tq, w), lambda b, h, i, lam: (b * nq + i, h)),
            scratch_shapes=[pltpu.VMEM((tk, tq), F32)] * nc + [pltpu.VMEM((tk, tq), BF16)] * nc,
        ),
        out_shape=jax.ShapeDtypeStruct((nb * t, heads * LANES), out_dtype),
        compiler_params=_cp(("parallel", "parallel", "arbitrary")),
        name="flash_da" if diff else "flash_mla",
    )(lam, q, k, vt)


def _page_copies(pt_ref, seq, slot, n_pages, pools, bufs, sems, layer):
    def per_page(p, fn):
        page = pt_ref[seq, p]
        window = pl.ds(pl.multiple_of(p * PAGE_SIZE, PAGE_SIZE), PAGE_SIZE)
        for pool, buf, sem in zip(pools, bufs, sems):
            assert buf.shape[1] != buf.shape[2]
            token_major = buf.shape[1] == n_pages * PAGE_SIZE
            dst = buf.at[slot, window] if token_major else buf.at[slot, :, window]
            fn(pltpu.make_async_copy(pool.at[layer, page], dst, sem.at[slot]))

    return per_page


def _fetch_pages(pt_ref, seq, slot, n_pages, pools, bufs, sems, layer):
    per_page = _page_copies(pt_ref, seq, slot, n_pages, pools, bufs, sems, layer)

    def body(p, c):
        per_page(p, lambda cp: cp.start())
        return c

    lax.fori_loop(0, n_pages, body, 0)


def _wait_pages(pt_ref, seq, slot, n_pages, pools, bufs, sems, layer):
    per_page = _page_copies(pt_ref, seq, slot, n_pages, pools, bufs, sems, layer)

    def body(p, c):
        per_page(p, lambda cp: cp.wait())
        return c

    lax.fori_loop(0, n_pages, body, 0)


def _paged_prologue(pt_ref, n_pages, pools, bufs, sems, layer):
    b = pl.program_id(0)
    nb = pl.num_programs(0)
    slot = b % 2

    @pl.when(b == 0)
    def _():
        _fetch_pages(pt_ref, 0, 0, n_pages, pools, bufs, sems, layer)

    @pl.when(b + 1 < nb)
    def _():
        _fetch_pages(pt_ref, b + 1, 1 - slot, n_pages, pools, bufs, sems, layer)

    _wait_pages(pt_ref, b, slot, n_pages, pools, bufs, sems, layer)
    return slot


def _paged_da_kernel(layer, n_pages, chunk, dec, pt_ref, lam_ref, q_ref, kn_ref, vn_ref, ktpool, vpool,
                     o_ref, ktbuf, vbuf, s_ref, ksem, vsem):
    slot = _paged_prologue(pt_ref, n_pages, (ktpool, vpool), (ktbuf, vbuf), (ksem, vsem), layer)
    tk = n_pages * PAGE_SIZE
    n_chunks = tk // chunk
    lam = lam_ref[0]

    q = q_ref[...].astype(F32)
    lane = _lane((dec, LANES))
    zero = jnp.zeros((dec, LANES), F32)
    pad = jnp.zeros((64 - DA_HEADS * dec, LANES), F32)
    rows = []
    for mp in range(2):
        keep = (lane < DA_HD) if mp == 0 else (lane >= DA_HD)
        rows += [jnp.where(keep, q[:, hh * LANES:(hh + 1) * LANES], zero) for hh in range(DA_HEADS)]
        rows.append(pad)
    qx = jnp.concatenate(rows, axis=0).astype(BF16)

    def lane_tiles(x, op):
        parts = [x[:, i * LANES:(i + 1) * LANES] for i in range(x.shape[1] // LANES)]
        while len(parts) > 1:
            parts = [op(parts[i], parts[i + 1]) for i in range(0, len(parts), 2)]
        return parts[0]

    def score(c, mx):
        off = pl.multiple_of(c * chunk, chunk)
        s = jnp.dot(qx, ktbuf[slot, :, pl.ds(off, chunk)].astype(BF16), preferred_element_type=F32)
        s_ref[:, pl.ds(off, chunk)] = s
        return jnp.maximum(mx, lane_tiles(s, jnp.maximum))

    mx = lax.fori_loop(0, n_chunks, score, jnp.full((LANES, LANES), NEG, F32))
    zrows = jnp.zeros((LANES - dec, LANES), F32)
    s_new = _dot_nt(qx, jnp.concatenate([kn_ref[...], zrows], axis=0))
    jcol = _lane((LANES, LANES))
    irow = lax.broadcasted_iota(jnp.int32, (LANES, LANES), 0) % dec
    s_new = jnp.where(jnp.logical_and(jcol < dec, jcol <= irow), s_new, NEG)
    m = jnp.max(jnp.maximum(mx, s_new), axis=-1, keepdims=True)
    p_new = jnp.exp2(s_new - m)

    def pv(c, carry):
        ls, acc = carry
        off = pl.multiple_of(c * chunk, chunk)
        p = jnp.exp2(s_ref[:, pl.ds(off, chunk)] - m)
        return ls + lane_tiles(p, jnp.add), acc + _dot(p, vbuf[slot, pl.ds(off, chunk), :])

    ls, acc = lax.fori_loop(0, n_chunks, pv,
                            (p_new, _dot(p_new, jnp.concatenate([vn_ref[...], zrows], axis=0))))
    o = acc / jnp.sum(ls, axis=-1, keepdims=True)
    half = LANES // 2
    o_ref[...] = jnp.concatenate(
        [o[hh * dec:(hh + 1) * dec, :] - lam * o[half + hh * dec:half + (hh + 1) * dec, :]
         for hh in range(DA_HEADS)], axis=-1)


def _paged_da(layer, pt, lam, q, k_new, v_new, ktpool, vpool, dec):
    ns, n_pages = pt.shape
    tk = n_pages * PAGE_SIZE
    chunk = min(1024, tk)
    seq = lambda b, pt, lam: (b, 0)
    return pl.pallas_call(
        functools.partial(_paged_da_kernel, layer, n_pages, chunk, dec),
        grid_spec=pltpu.PrefetchScalarGridSpec(
            num_scalar_prefetch=2,
            grid=(ns,),
            in_specs=[pl.BlockSpec((dec, 512), seq), pl.BlockSpec((dec, LANES), seq),
                      pl.BlockSpec((dec, LANES), seq),
                      pl.BlockSpec(memory_space=pl.ANY), pl.BlockSpec(memory_space=pl.ANY)],
            out_specs=pl.BlockSpec((dec, 512), seq),
            scratch_shapes=[pltpu.VMEM((2, LANES, tk), F32), pltpu.VMEM((2, tk, LANES), F32),
                            pltpu.VMEM((LANES, tk), F32),
                            pltpu.SemaphoreType.DMA((2,)), pltpu.SemaphoreType.DMA((2,))],
        ),
        out_shape=jax.ShapeDtypeStruct((ns * dec, 512), F32),
        compiler_params=_cp(("arbitrary",)),
        name="paged_da",
    )(pt, lam, q, k_new, v_new, ktpool, vpool)


def _paged_mla_kernel(layer, n_pages, chunk, dec, pt_ref, q_ref, cn_ref, rn_ref, wuk_ref, wuv_ref, gkn_ref,
                      seln_ref, selr_ref, fold_ref, foldt_ref, expand_ref, cpool, rpool,
                      o_ref, cbuf, rbuf, s_ref, csem, rsem):
    slot = _paged_prologue(pt_ref, n_pages, (cpool, rpool), (cbuf, rbuf), (csem, rsem), layer)
    tk = n_pages * PAGE_SIZE
    n_chunks = tk // chunk
    nrow = MLA_HEADS * dec

    q = q_ref[...]
    qn = jnp.dot(q, seln_ref[...], preferred_element_type=F32) * gkn_ref[...]
    qr = jnp.dot(q, selr_ref[...], preferred_element_type=F32)
    reps = LANES // dec
    rowh = lax.broadcasted_iota(jnp.int32, (LANES, 1), 0) // dec
    qn_t = jnp.concatenate([qn] * reps, axis=0)
    qx = jnp.where(_lane(qn_t.shape) // MLA_NOPE == rowh, qn_t, 0.0).astype(BF16)
    qr_t = jnp.concatenate([qr] * reps, axis=0)
    qr_m = jnp.where(_lane(qr_t.shape) // MLA_ROPE == rowh, qr_t, 0.0).astype(BF16)
    qrx = jnp.dot(qr_m, fold_ref[...], preferred_element_type=F32).astype(BF16)
    qrx_t = _dot_nt(foldt_ref[...], qr_m).astype(BF16)
    wuk = wuk_ref[...]
    expand = expand_ref[...]
    w_all = jnp.concatenate([wuk, _dot_nt(wuk, qx).astype(BF16)], axis=1)
    n_kn = MLA_HEADS * MLA_NOPE

    def scores(c_lat, rope_term):
        both = _dot(c_lat, w_all)
        kn = both[:, :n_kn]
        ssq = _dot(kn * kn, expand)
        return both[:, n_kn:] * lax.rsqrt(ssq / MLA_NOPE + NORM_EPS) + rope_term

    def score(c, mx8):
        off = pl.multiple_of(c * chunk, chunk)
        rope_term = _dot_tn(rbuf[slot, :, pl.ds(off, chunk)], qrx_t)
        s = scores(cbuf[slot, pl.ds(off, chunk), :], rope_term)
        s_ref[pl.ds(off, chunk), :] = s
        return jnp.maximum(mx8, jnp.max(s.reshape(chunk // SUB, SUB, LANES), axis=0))

    mx8 = lax.fori_loop(0, n_chunks, score, jnp.full((SUB, LANES), NEG, F32))
    c_new = cn_ref[...]
    s_new = scores(c_new, _dot_nt(rn_ref[...], qrx))
    jrow = lax.broadcasted_iota(jnp.int32, (dec, LANES), 0)
    icol = _lane((dec, LANES)) % dec
    s_new = jnp.where(jrow <= icol, s_new, NEG)
    m = jnp.maximum(jnp.max(mx8, axis=0, keepdims=True), jnp.max(s_new, axis=0, keepdims=True))
    p_new = jnp.exp2(s_new - m)

    def pv(c, carry):
        l, acc = carry
        off = pl.multiple_of(c * chunk, chunk)
        p = jnp.exp2(s_ref[pl.ds(off, chunk), :] - m)
        return (l + jnp.sum(p, axis=0, keepdims=True),
                acc + _dot_tn(p, cbuf[slot, pl.ds(off, chunk), :]))

    l, acc = lax.fori_loop(0, n_chunks, pv,
                           (jnp.sum(p_new, axis=0, keepdims=True), _dot_tn(p_new, c_new)))
    eye = (lax.broadcasted_iota(jnp.int32, (LANES, LANES), 0) == _lane((LANES, LANES))).astype(F32)
    l_col = jnp.sum(eye * l, axis=-1, keepdims=True)
    o_lat = acc / l_col
    wuv = wuv_ref[...]
    outs = []
    for hh in range(MLA_HEADS):
        outs.append(_dot(o_lat[hh * dec:(hh + 1) * dec, :], wuv[:, hh * MLA_V:(hh + 1) * MLA_V]))
    o_ref[...] = jnp.concatenate(outs, axis=-1).astype(o_ref.dtype)
    del nrow


def _paged_mla(layer, pt, q, c_new, r_new, lw, consts, cpool, rpool, dec):
    ns, n_pages = pt.shape
    tk = n_pages * PAGE_SIZE
    chunk = min(1024, tk)
    seq = lambda b, pt: (b, 0)
    const2 = lambda b, pt: (0, 0)
    small = [lw["w_uk_c"], lw["w_uv_c"], lw["g_kn_t"], consts["sel_nope"], consts["sel_rope"],
             consts["fold_rope"], consts["fold_rope_t"], consts["expand_head"]]
    return pl.pallas_call(
        functools.partial(_paged_mla_kernel, layer, n_pages, chunk, dec),
        grid_spec=pltpu.PrefetchScalarGridSpec(
            num_scalar_prefetch=1,
            grid=(ns,),
            in_specs=[pl.BlockSpec((dec, 1024), seq), pl.BlockSpec((dec, MLA_KV_LORA), seq),
                      pl.BlockSpec((dec, MLA_ROPE), seq)]
                     + [pl.BlockSpec(a.shape, const2) for a in small]
                     + [pl.BlockSpec(memory_space=pl.ANY), pl.BlockSpec(memory_space=pl.ANY)],
            out_specs=pl.BlockSpec((dec, MLA_HEADS * MLA_V), seq),
            scratch_shapes=[pltpu.VMEM((2, tk, MLA_KV_LORA), F32), pltpu.VMEM((2, MLA_ROPE, tk), F32),
                            pltpu.VMEM((tk, LANES), F32),
                            pltpu.SemaphoreType.DMA((2,)), pltpu.SemaphoreType.DMA((2,))],
        ),
        out_shape=jax.ShapeDtypeStruct((ns * dec, MLA_HEADS * MLA_V), BF16),
        compiler_params=_cp(("arbitrary",)),
        name="paged_mla",
    )(pt, q, c_new, r_new, *small, cpool, rpool)


def _conv_kernel(t, tt, ext_ref, w_ref, b_ref, g_ref, bb_ref, o_ref):
    w = w_ref[...]

    def tile(i, carry):
        base = pl.multiple_of(i * tt, tt)
        win = ext_ref[0, pl.ds(base, tt + CONV_W - 1), :]
        acc = jnp.zeros((tt, CONV_C), F32)
        for j in range(CONV_W):
            acc = acc + win[j:j + tt, :] * w[j:j + 1, :]
        y = acc + b_ref[...]
        mu = jnp.mean(y, axis=-1, keepdims=True)
        yc = y - mu
        yn = yc * lax.rsqrt(jnp.mean(yc * yc, axis=-1, keepdims=True) + NORM_EPS) * g_ref[...] + bb_ref[...]
        o_ref[0, pl.ds(base, tt), :] = (yn * jax.nn.sigmoid(yn)).astype(o_ref.dtype)
        return carry

    lax.fori_loop(0, t // tt, tile, 0)


def _conv(ext, lw, t):
    nb = ext.shape[0]
    tt = min(32, t)
    vec = pl.BlockSpec((1, CONV_C), lambda b: (0, 0))
    return pl.pallas_call(
        functools.partial(_conv_kernel, t, tt),
        grid=(nb,),
        in_specs=[pl.BlockSpec((1, CONV_W - 1 + t, CONV_C), lambda b: (b, 0, 0)),
                  pl.BlockSpec((CONV_W, CONV_C), lambda b: (0, 0)), vec, vec, vec],
        out_specs=pl.BlockSpec((1, t, CONV_C), lambda b: (b, 0, 0)),
        out_shape=jax.ShapeDtypeStruct((nb, t, CONV_C), BF16),
        compiler_params=_cp(("parallel",)),
        name="conv",
    )(ext, lw["cv_w"], lw["cv_b"], lw["cv_g"], lw["cv_bb"])


def _merge_route_kernel(lam_scale, x_ref, oda_ref, ocv_ref, omla_ref, gate_ref, g1_ref, sc2_ref, sh2_ref,
                        gout_ref, gffn_ref, wb0_ref, wb1_ref, wb2_ref, wout_ref, wr_ref, br_ref, tri_ref,
                        xo_ref, h_ref, mi_ref, mf_ref, cnt_ref, carry_ref):
    first = jnp.logical_and(pl.program_id(0) == 0, pl.program_id(1) == 0)

    @pl.when(first)
    def _():
        carry_ref[...] = jnp.zeros_like(carry_ref)

    oda = oda_ref[...]
    gout = gout_ref[...]
    das = []
    for hh in range(DA_HEADS):
        z = oda[:, hh * LANES:(hh + 1) * LANES]
        r = lax.rsqrt(jnp.mean(z * z, axis=-1, keepdims=True) + NORM_EPS)
        das.append((z * r * gout * lam_scale).astype(BF16))
    o_da = jnp.concatenate(das, axis=-1)
    gate = gate_ref[...].astype(F32)
    merged = (gate[:, :D_MODEL] * jnp.dot(o_da, wb0_ref[...], preferred_element_type=F32)
              + gate[:, D_MODEL:2 * D_MODEL] * jnp.dot(ocv_ref[...], wb1_ref[...], preferred_element_type=F32)
              + gate[:, 2 * D_MODEL:] * jnp.dot(omla_ref[...], wb2_ref[...], preferred_element_type=F32))
    x = x_ref[...] + g1_ref[0] * _dot(merged, wout_ref[...])
    xo_ref[...] = x
    xn = x * lax.rsqrt(jnp.mean(x * x, axis=-1, keepdims=True) + NORM_EPS) * gffn_ref[...]
    h = xn * (1.0 + sc2_ref[0]) + sh2_ref[0]
    _rows_to_tiles(h_ref, h)

    tm = x.shape[0]
    logits = _dot(h, wr_ref[...]) + br_ref[...]
    lane = _lane((tm, LANES))
    work = logits
    sel = jnp.zeros((tm, LANES), F32)
    vals, idxs = [], []
    for _ in range(TOP_K):
        mx = jnp.max(work, axis=-1, keepdims=True)
        idx = jnp.min(jnp.where(work == mx, lane, LANES), axis=-1, keepdims=True)
        hit = lane == idx
        sel = jnp.where(hit, 1.0, sel)
        work = jnp.where(hit, NEG * 2.0, work)
        vals.append(mx)
        idxs.append(idx)
    es = [jnp.exp(v - vals[0]) for v in vals]
    den = es[0] + es[1] + es[2] + es[3]
    rank_all = carry_ref[...] + jnp.dot(tri_ref[...], sel.astype(BF16), preferred_element_type=F32)
    carry_ref[...] = carry_ref[...] + jnp.sum(sel, axis=0, keepdims=True)
    mi = jnp.zeros((tm, LANES), jnp.int32)
    mf = jnp.zeros((tm, LANES), F32)
    for kk in range(TOP_K):
        rk = jnp.sum(jnp.where(lane == idxs[kk], rank_all, 0.0), axis=-1, keepdims=True).astype(jnp.int32)
        mi = jnp.where(lane == kk, idxs[kk], mi)
        mi = jnp.where(lane == TOP_K + kk, rk, mi)
        mf = jnp.where(lane == kk, es[kk] / den, mf)
    mi_ref[...] = mi
    mf_ref[...] = mf
    cnt_ref[...] = carry_ref[...]


def _merge_route(x, oda, ocv, omla, gate, g1, sc2, sh2, lw, lam_scale, nt, nb, tm):
    n = x.shape[0]
    r = g1.shape[1]
    row = lambda t, b: (b * nt + t, 0)
    const2 = lambda t, b: (0, 0)
    mod_spec = pl.BlockSpec((1, r, D_MODEL), lambda t, b: (b, 0, 0))
    tri = (np.arange(tm)[:, None] > np.arange(tm)[None, :]).astype(np.float32)
    tri = jnp.asarray(tri, BF16)
    small = [lw["g_out"], lw["g_ffn"], lw["w_b0"], lw["w_b1"], lw["w_b2"], lw["w_out"], lw["w_router"],
             lw["b_router"], tri]

    def rows(w):
        return pl.BlockSpec((tm, w), row)

    return pl.pallas_call(
        functools.partial(_merge_route_kernel, lam_scale),
        grid=(nt, nb),
        in_specs=[rows(D_MODEL), rows(512), rows(CONV_C), rows(omla.shape[1]), rows(N_BRANCH * D_MODEL),
                  mod_spec, mod_spec, mod_spec] + [pl.BlockSpec(a.shape, const2) for a in small],
        out_specs=[rows(D_MODEL), pl.BlockSpec((tm * SUB, LANES), row), rows(LANES), rows(LANES),
                   pl.BlockSpec((1, LANES), const2)],
        out_shape=[jax.ShapeDtypeStruct((n, D_MODEL), F32), jax.ShapeDtypeStruct((n * SUB, LANES), F32),
                   jax.ShapeDtypeStruct((n, LANES), jnp.int32), jax.ShapeDtypeStruct((n, LANES), F32),
                   jax.ShapeDtypeStruct((1, LANES), F32)],
        scratch_shapes=[pltpu.VMEM((1, LANES), F32)],
        compiler_params=_cp(("arbitrary", "arbitrary")),
        name="merge_route",
    )(x, oda, ocv, omla, gate, g1, sc2, sh2, *small)


def _row_copy(src, row, buf, slot, r, sem):
    return pltpu.make_async_copy(src.at[pl.ds(pl.multiple_of(row * SUB, SUB), SUB)],
                                 buf.at[slot, pl.ds(pl.multiple_of(r * SUB, SUB), SUB)], sem.at[slot])


def _experts_kernel(bm, be_ref, nblk_ref, tok0_ref, tokn_ref, h_hbm, wup_ref, bup_ref, wdn_ref, bdn_ref,
                    y_ref, xbuf, sem):
    i = pl.program_id(0)
    nblk = nblk_ref[0]
    slot = i % 2

    def fetch(tok_ref, sl):
        def body(r, c):
            _row_copy(h_hbm, tok_ref[0, 0, r], xbuf, sl, r, sem).start()
            return c

        lax.fori_loop(0, bm, body, 0, unroll=8)

    @pl.when(jnp.logical_and(i == 0, nblk > 0))
    def _():
        fetch(tok0_ref, 0)

    @pl.when(i + 1 < nblk)
    def _():
        fetch(tokn_ref, 1 - slot)

    @pl.when(i < nblk)
    def _():
        def wait_row(r, c):
            _row_copy(h_hbm, 0, xbuf, slot, r, sem).wait()
            return c

        lax.fori_loop(0, bm, wait_row, 0, unroll=8)
        xg = _tiles_to_rows(xbuf, slot, 0, bm).astype(BF16)
        u = jnp.dot(xg, wup_ref[0], preferred_element_type=F32) + bup_ref[0]
        g = jnp.minimum(u[:, :D_FF], SWIGLU_LIMIT)
        lin = jnp.clip(u[:, D_FF:], -SWIGLU_LIMIT, SWIGLU_LIMIT)
        act = g * jax.nn.sigmoid(SWIGLU_ALPHA * g) * (lin + 1.0)
        _rows_to_tiles(y_ref, _dot(act, wdn_ref[0]) + bdn_ref[0])

    @pl.when(i >= nblk)
    def _():
        y_ref[...] = jnp.zeros_like(y_ref)


def _experts(block_e, nblk, slot_tok, h, lw, bm, n_blocks):
    def wmap(i, be, nb):
        return (be[i], 0, 0)

    tok_shape = (1, 1, bm)
    return pl.pallas_call(
        functools.partial(_experts_kernel, bm),
        grid_spec=pltpu.PrefetchScalarGridSpec(
            num_scalar_prefetch=2,
            grid=(n_blocks,),
            in_specs=[pl.BlockSpec(tok_shape, lambda i, be, nb: (0, 0, 0), memory_space=pltpu.SMEM),
                      pl.BlockSpec(tok_shape, lambda i, be, nb: (jnp.minimum(i + 1, n_blocks - 1), 0, 0),
                                   memory_space=pltpu.SMEM),
                      pl.BlockSpec(memory_space=pl.ANY),
                      pl.BlockSpec((1, D_MODEL, 2 * D_FF), wmap), pl.BlockSpec((1, 1, 2 * D_FF), wmap),
                      pl.BlockSpec((1, D_FF, D_MODEL), wmap), pl.BlockSpec((1, 1, D_MODEL), wmap)],
            out_specs=pl.BlockSpec((bm * SUB, LANES), lambda i, be, nb: (i, 0)),
            scratch_shapes=[pltpu.VMEM((2, bm * SUB, LANES), F32), pltpu.SemaphoreType.DMA((2,))],
        ),
        out_shape=jax.ShapeDtypeStruct((n_blocks * bm * SUB, LANES), F32),
        compiler_params=_cp(("arbitrary",)),
        name="experts",
    )(block_e, nblk, slot_tok, slot_tok, h, lw["w_up"], lw["b_up"], lw["w_down"], lw["b_down"])


def _combine_kernel(tm, d0_ref, dn_ref, x_ref, mf_ref, g2_ref, yb_hbm, o_ref, ybuf, sem):
    i = pl.program_id(0)
    n = pl.num_programs(0)
    slot = i % 2
    rows = tm * TOP_K

    def fetch(d_ref, sl):
        def body(r, c):
            _row_copy(yb_hbm, d_ref[0, 0, r], ybuf, sl, r, sem).start()
            return c

        lax.fori_loop(0, rows, body, 0, unroll=8)

    @pl.when(i == 0)
    def _():
        fetch(d0_ref, 0)

    @pl.when(i + 1 < n)
    def _():
        fetch(dn_ref, 1 - slot)

    def wait_row(r, c):
        _row_copy(yb_hbm, 0, ybuf, slot, r, sem).wait()
        return c

    lax.fori_loop(0, rows, wait_row, 0, unroll=8)
    mf = mf_ref[...]
    y = jnp.zeros((tm, D_MODEL), F32)
    for kk in range(TOP_K):
        y = y + _tiles_to_rows(ybuf, slot, kk * tm, tm) * mf[:, kk:kk + 1]
    o_ref[...] = x_ref[...] + g2_ref[0] * y


def _combine(dest, x, mf, g2, yb, tm, tiles_per_mod):
    n = x.shape[0]
    n_tiles = n // tm
    r = g2.shape[1]
    row = lambda i: (i, 0)
    d_shape = (1, 1, tm * TOP_K)
    return pl.pallas_call(
        functools.partial(_combine_kernel, tm),
        grid=(n_tiles,),
        in_specs=[pl.BlockSpec(d_shape, lambda i: (0, 0, 0), memory_space=pltpu.SMEM),
                  pl.BlockSpec(d_shape, lambda i: (jnp.minimum(i + 1, n_tiles - 1), 0, 0),
                               memory_space=pltpu.SMEM),
                  pl.BlockSpec((tm, D_MODEL), row), pl.BlockSpec((tm, LANES), row),
                  pl.BlockSpec((1, r, D_MODEL), lambda i: (i // tiles_per_mod, 0, 0)),
                  pl.BlockSpec(memory_space=pl.ANY)],
        out_specs=pl.BlockSpec((tm, D_MODEL), row),
        out_shape=jax.ShapeDtypeStruct((n, D_MODEL), F32),
        scratch_shapes=[pltpu.VMEM((2, TOP_K * tm * SUB, LANES), F32), pltpu.SemaphoreType.DMA((2,))],
        compiler_params=_cp(("arbitrary",)),
        name="combine",
    )(dest, dest, x, mf, g2, yb)


def _rope_cs(pos, rot, theta):
    inv = 1.0 / (theta ** (jnp.arange(0, rot, 2, dtype=F32) / rot))
    ang = pos.astype(F32)[:, None] * inv[None, :]
    return jnp.cos(ang), jnp.sin(ang)


def _lane_tables(pos):
    t = pos.shape[0]
    c, s = _rope_cs(pos, DA_ROT, DA_THETA)
    h = DA_ROT // 2
    one = jnp.ones((t, DA_HD - DA_ROT), F32)
    zero = lambda w: jnp.zeros((t, w), F32)
    da_c = jnp.concatenate([c, c, one], axis=1)
    da_sa = jnp.concatenate([-s, zero(DA_HD - h)], axis=1)
    da_sb = jnp.concatenate([zero(h), s, zero(DA_HD - DA_ROT)], axis=1)
    da = [jnp.concatenate([a, a], axis=1) for a in (da_c, da_sa, da_sb)]
    c, s = _rope_cs(pos, MLA_ROPE, MLA_THETA)
    h = MLA_ROPE // 2
    tail = LANES - MLA_NOPE - MLA_ROPE
    ml_c = jnp.concatenate([jnp.ones((t, MLA_NOPE), F32), c, c, jnp.ones((t, tail), F32)], axis=1)
    ml_sa = jnp.concatenate([zero(MLA_NOPE), -s, zero(h + tail)], axis=1)
    ml_sb = jnp.concatenate([zero(MLA_NOPE + h), s, zero(tail)], axis=1)
    return da + [ml_c, ml_sa, ml_sb]


def _selection_constants(dec):
    sel_nope = np.zeros((MLA_HEADS * LANES, MLA_HEADS * MLA_NOPE), np.float32)
    sel_rope = np.zeros((MLA_HEADS * LANES, MLA_HEADS * MLA_ROPE), np.float32)
    fold = np.zeros((MLA_HEADS * MLA_ROPE, MLA_ROPE), np.float32)
    expand = np.zeros((MLA_HEADS * MLA_NOPE, LANES), np.float32)
    for h in range(MLA_HEADS):
        for d in range(MLA_NOPE):
            sel_nope[h * LANES + d, h * MLA_NOPE + d] = 1.0
            expand[h * MLA_NOPE + d, h * dec:(h + 1) * dec] = 1.0
        for d in range(MLA_ROPE):
            sel_rope[h * LANES + MLA_NOPE + d, h * MLA_ROPE + d] = 1.0
            fold[h * MLA_ROPE + d, d] = 1.0
    return {"sel_nope": jnp.asarray(sel_nope, BF16), "sel_rope": jnp.asarray(sel_rope, BF16),
            "fold_rope": jnp.asarray(fold, BF16), "fold_rope_t": jnp.asarray(fold.T.copy(), BF16),
            "expand_head": jnp.asarray(expand, BF16)}


def _pad_heads(w, real):
    pad = [(0, 0)] * (w.ndim - 1) + [(0, LANES - real)]
    return jnp.pad(w, pad).reshape(w.shape[:-2] + (w.shape[-2] * LANES,))


def _prep_weights(P):
    L = DEPTH
    splits = np.cumsum(IN_WIDTHS)[:-1].tolist()
    wq, wk, wv, wglu, wqa, wkva, wkr, wgate = jnp.split(P["w_in"], splits, axis=-1)
    wkr = jnp.pad(wkr, ((0, 0), (0, 0), (MLA_NOPE, LANES - MLA_NOPE - MLA_ROPE)))
    w_in = jnp.concatenate([wq, wk, wv, wglu, wqa, wkva, wkr, wgate], axis=-1).astype(BF16)
    w_qb = _pad_heads(P["mla_w_qb"].reshape(L, MLA_Q_LORA, MLA_HEADS, MLA_NOPE + MLA_ROPE),
                      MLA_NOPE + MLA_ROPE).astype(BF16)
    w_uk = _pad_heads(P["mla_w_uk"], MLA_NOPE).astype(BF16)
    w_uv = _pad_heads(P["mla_w_uv"], MLA_V).astype(BF16)
    w_uk_c = P["mla_w_uk"].reshape(L, MLA_KV_LORA, MLA_HEADS * MLA_NOPE).astype(BF16)
    w_uv_c = P["mla_w_uv"].reshape(L, MLA_KV_LORA, MLA_HEADS * MLA_V).astype(BF16)
    wb = P["w_branch"].astype(BF16)
    w_b2p = jnp.pad(wb[:, 2].reshape(L, MLA_HEADS, MLA_V, D_MODEL),
                    ((0, 0), (0, 0), (0, LANES - MLA_V), (0, 0))).reshape(L, MLA_HEADS * LANES, D_MODEL)
    z = lambda w: jnp.zeros((L, w), F32)
    g_qm = jnp.concatenate([P["mla_g_qn"], P["mla_g_qr"], z(LANES - MLA_NOPE - MLA_ROPE)], axis=-1)
    g_kr = jnp.concatenate([z(MLA_NOPE), P["mla_g_kr"], z(LANES - MLA_NOPE - MLA_ROPE)], axis=-1)
    g_kn = jnp.concatenate([P["mla_g_kn"], z(LANES - MLA_NOPE)], axis=-1)
    w_router = jnp.pad(P["moe_w_router"], ((0, 0), (0, 0), (0, LANES - N_EXPERTS))).astype(BF16)
    b_router = jnp.concatenate([P["moe_b_router"].astype(F32), jnp.full((L, LANES - N_EXPERTS), NEG, F32)],
                               axis=-1)
    w_up = P["moe_w_up"].astype(BF16)
    w_down = P["moe_w_down"].astype(BF16)
    w_out = P["w_out"].astype(BF16)
    lp = P["da_lambda"].astype(F32)
    lam_dyn = jnp.exp(jnp.sum(lp[:, 0] * lp[:, 1], axis=-1)) - jnp.exp(jnp.sum(lp[:, 2] * lp[:, 3], axis=-1))
    row = lambda a: a[None, :]
    layers = []
    for l in range(L):
        lam_init = 0.8 - 0.6 * math.exp(-0.3 * l)
        layers.append({
            "lam": (lam_dyn[l] + lam_init).reshape(1), "lam_scale": 1.0 - lam_init,
            "g_mix": row(P["g_norm_mix"][l]), "g_ffn": row(P["g_norm_ffn"][l]),
            "w_in": w_in[l], "g_q": P["da_g_q"][l].reshape(1, LANES), "g_k": P["da_g_k"][l].reshape(1, LANES),
            "g_qa": row(P["mla_g_qa"][l]), "g_kva": row(P["mla_g_kva"][l]), "g_kr": row(g_kr[l]),
            "g_qm": row(g_qm[l]), "g_kn": row(g_kn[l]), "g_kn_t": row(jnp.tile(P["mla_g_kn"][l], MLA_HEADS)),
            "w_qb": w_qb[l], "w_uk": w_uk[l], "w_uv_t": w_uv[l].T, "w_v_t": wv[l].astype(BF16).T,
            "w_uk_c": w_uk_c[l], "w_uv_c": w_uv_c[l],
            "cv_w": P["cv_w_dw"][l], "cv_b": row(P["cv_b_dw"][l]), "cv_g": row(P["cv_ln_g"][l]),
            "cv_bb": row(P["cv_ln_b"][l]),
            "g_out": row(P["da_g_out"][l]), "w_b0": wb[l, 0], "w_b1": wb[l, 1], "w_b2_c": wb[l, 2],
            "w_b2_p": w_b2p[l], "w_out": w_out[l], "w_router": w_router[l], "b_router": row(b_router[l]),
            "w_up": w_up[l], "b_up": P["moe_b_up"][l][:, None, :], "w_down": w_down[l],
            "b_down": P["moe_b_down"][l][:, None, :],
        })
    return layers


def _moe_dispatch(mi, cnt, n, bm):
    e = mi[:, :TOP_K]
    rank = mi[:, TOP_K:2 * TOP_K]
    counts = cnt[0, :N_EXPERTS].astype(jnp.int32)
    padded = (counts + bm - 1) // bm * bm
    pend = jnp.cumsum(padded)
    pstart = pend - padded
    dest = pstart[e] + rank
    n_blocks = -(-(n * TOP_K) // bm) + N_EXPERTS
    tok = jnp.broadcast_to(jnp.arange(n, dtype=jnp.int32)[:, None], (n, TOP_K))
    slot_tok = jnp.zeros((n_blocks * bm,), jnp.int32).at[dest.reshape(-1)].set(
        tok.reshape(-1), unique_indices=True, mode="promise_in_bounds")
    block_e = jnp.minimum(jnp.searchsorted(pend, jnp.arange(n_blocks, dtype=jnp.int32) * bm, side="right"),
                          N_EXPERTS - 1).astype(jnp.int32)
    nblk = (pend[-1] // bm).astype(jnp.int32).reshape(1)
    return dest, slot_tok.reshape(n_blocks, 1, bm), block_e, nblk, n_blocks


def _trunk(x3, mod, layers, consts, past):
    b, t, _ = x3.shape
    n = b * t
    x = x3.reshape(n, D_MODEL)
    if past is None:
        start = 0
        tm = min(256, t)
        nt, nb, r, tiles_per_mod = t // tm, b, 1, t // tm
    else:
        kpool, vpool, cpool, rpool, sconv, pt = past
        start = pt.shape[1] * PAGE_SIZE
        tm = min(256, n)
        nt, nb, r, tiles_per_mod = 1, n // tm, tm, 1
    pos = start + jnp.arange(t, dtype=jnp.int32)
    tabs = _lane_tables(pos)
    if past is None:
        tabs = [a.reshape(nt, tm, LANES) for a in tabs]
    else:
        tabs = [jnp.tile(a, (tm // t, 1)).reshape(1, tm, LANES) for a in tabs]
    bm = 512 if n * TOP_K >= 32768 else 128

    def rows(a):
        if past is None:
            return a.reshape(b, 1, D_MODEL)
        return jnp.repeat(a, t, axis=0).reshape(nb, tm, D_MODEL)

    states = []
    for l, lw in enumerate(layers):
        sh1, sc1, g1, sh2, sc2, g2 = [rows(a) for a in jnp.split(mod[l], 6, axis=-1)]
        outs = _token_prep(x, sc1, sh1, tabs, lw, nt, nb, tm, past is None)
        qda, k, v, u, qmla, ckv, kr, gate = outs[:8]
        u3 = u.reshape(b, t, CONV_C)
        if past is None:
            kcat, vht, kb, vbt = outs[8:]
            oda = _flash(lw["lam"], qda, kb, vbt, b, t, DA_HEADS, True, True, F32)
            omla = _flash(lw["lam"], qmla, kcat, vht, b, t, MLA_HEADS, False, False, BF16)
            w_b2 = lw["w_b2_p"]
            prev = jnp.zeros((b, CONV_W - 1, CONV_C), F32)
        else:
            oda = _paged_da(l, pt, lw["lam"], qda, k, v, kpool, vpool, t)
            omla = _paged_mla(l, pt, qmla, ckv, kr, lw, consts, cpool, rpool, t)
            w_b2 = lw["w_b2_c"]
            prev = sconv[l]
        ext = jnp.concatenate([prev, u3], axis=1)
        ocv = _conv(ext, lw, t).reshape(n, CONV_C)
        lw2 = dict(lw, w_b2=w_b2)
        x, h, mi, mf, cnt = _merge_route(x, oda, ocv, omla, gate, g1, sc2, sh2, lw2, lw["lam_scale"],
                                         nt, nb, tm)
        dest, slot_tok, block_e, nblk, n_blocks = _moe_dispatch(mi, cnt, n, bm)
        yb = _experts(block_e, nblk, slot_tok, h, lw, bm, n_blocks)
        dest_t = dest.reshape(n // tm, tm, TOP_K).transpose(0, 2, 1).reshape(n // tm, 1, TOP_K * tm)
        x = _combine(dest_t, x, mf, g2, yb, tm, tiles_per_mod)
        states.append((k.reshape(b, t, 2, DA_HD), v.reshape(b, t, DA_VD), ckv.reshape(b, t, MLA_KV_LORA),
                       kr.reshape(b, t, MLA_ROPE), ext[:, -(CONV_W - 1):]))
    stacked = [jnp.stack(s, axis=0) for s in zip(*states)]
    return x.reshape(b, t, D_MODEL), stacked


def kernel(x_prompt, x_sample, c_prompt, c_sample, cache_da_k, cache_da_v, cache_mla_ckv, cache_mla_krope,
           state_conv, page_table, w_ada, b_ada, g_norm_mix, g_norm_ffn, w_in, da_g_q, da_g_k, da_lambda,
           da_g_out, cv_w_dw, cv_b_dw, cv_ln_g, cv_ln_b, mla_g_qa, mla_w_qb, mla_g_kva, mla_w_uk, mla_w_uv,
           mla_g_qn, mla_g_kn, mla_g_qr, mla_g_kr, w_branch, w_out, moe_w_router, moe_b_router, moe_w_up,
           moe_b_up, moe_w_down, moe_b_down):
    P = dict(w_in=w_in, da_g_q=da_g_q, da_g_k=da_g_k, da_lambda=da_lambda, da_g_out=da_g_out,
             g_norm_mix=g_norm_mix, g_norm_ffn=g_norm_ffn,
             cv_w_dw=cv_w_dw, cv_b_dw=cv_b_dw, cv_ln_g=cv_ln_g, cv_ln_b=cv_ln_b,
             mla_g_qa=mla_g_qa, mla_w_qb=mla_w_qb, mla_g_kva=mla_g_kva, mla_w_uk=mla_w_uk, mla_w_uv=mla_w_uv,
             mla_g_qn=mla_g_qn, mla_g_kn=mla_g_kn, mla_g_qr=mla_g_qr, mla_g_kr=mla_g_kr,
             w_branch=w_branch, w_out=w_out, moe_w_router=moe_w_router, moe_b_router=moe_b_router,
             moe_w_up=moe_w_up, moe_b_up=moe_b_up, moe_w_down=moe_w_down, moe_b_down=moe_b_down)
    layers = _prep_weights(P)
    nbp = c_prompt.shape[0]
    mod = _ada(jnp.concatenate([c_prompt, c_sample], axis=0), w_ada, b_ada)
    dec = x_sample.shape[1]
    consts = _selection_constants(dec)
    ktpool = jnp.transpose(cache_da_k, (0, 1, 3, 4, 2)).reshape(cache_da_k.shape[:2] + (2 * DA_HD, PAGE_SIZE))
    rtpool = jnp.transpose(cache_mla_krope, (0, 1, 3, 2))
    past = (ktpool, cache_da_v, cache_mla_ckv, rtpool, state_conv, page_table)
    y_sample, (sk, sv, sc, sr, ss) = _trunk(x_sample, mod[:, nbp:], layers, consts, past)
    y_prompt, (pk, pv, pc, pr, ps) = _trunk(x_prompt, mod[:, :nbp], layers, consts, None)
    return (y_prompt, y_sample, pk, pv, pc, pr, ps, sk, sv, sc, sr, ss)
```
